```python
import jax, jax.numpy as jnp
from jax import lax
import numpy as np

D_MODEL = 1024
BATCH = 16
SEQ = 2048
DEPTH = 2
DEC_BATCH = 16
DEC_SEQ = 16
PAST_LEN = 1024

CHUNK = 64
N_A = DEPTH // 2
N_B = DEPTH - N_A
HEAD_A = 64
H_A = D_MODEL // HEAD_A
LORA_DECAY = 64
LORA_AAA = 64
LORA_GATE = 128
HEAD_B = 64
H_B = D_MODEL // HEAD_B
D_B = H_B * HEAD_B
D_FF = -(-8 * D_MODEL // (3 * 256)) * 256
QBLK = 128
RMS_EPS = 1e-5
GN_EPS = 64e-5
SCALE = HEAD_B ** -0.5
F32 = jnp.float32

kernel_name = 'rwkv7_fox_yoco_stream_step'


def rmsnorm(x, g):
    xf = x.astype(F32)
    y = xf * lax.rsqrt(jnp.mean(xf * xf, axis=-1, keepdims=True) + RMS_EPS)
    return (y * g.astype(F32)).astype(x.dtype)


def swiglu(x, w_gate, w_up, w_down):
    return (jax.nn.silu(x @ w_gate) * (x @ w_up)) @ w_down


def wkv7_scan(r, w, k, v, a, b, S0):
    def step(S, inp):
        r_t, w_t, k_t, v_t, a_t, b_t = inp
        sa = jnp.einsum('bhij,bhj->bhi', S, a_t)
        S = S * w_t[:, :, None, :] + sa[..., None] * b_t[:, :, None, :] + v_t[..., None] * k_t[:, :, None, :]
        y = jnp.einsum('bhij,bhj->bhi', S, r_t)
        return S, y
    xs = tuple(jnp.moveaxis(t, 1, 0) for t in (r, w, k, v, a, b))
    S, ys = lax.scan(step, S0, xs)
    return jnp.moveaxis(ys, 0, 1), S


def rwkv7_time_mix(x, shift0, S0, mu, w_r, w_k, w_v, w_o, w0, w1, w2, a0, a1, a2,
                   g1, g2, k_k, k_a, r_k, lnx_g, lnx_b):
    B, T, D = x.shape
    x_prev = jnp.concatenate([shift0[:, None, :].astype(x.dtype), x[:, :-1]], axis=1)
    xx = x_prev - x
    xr, xw, xk, xv, xa, xg = (x + xx * mu[i] for i in range(6))
    r = xr @ w_r
    k = xk @ w_k
    v = xv @ w_v
    w_log = -jax.nn.softplus(-(w0 + jnp.tanh(xw @ w1) @ w2).astype(F32)) - 0.5
    decay = jnp.exp(-jnp.exp(w_log))
    a = jax.nn.sigmoid((a0 + (xa @ a1) @ a2).astype(F32))
    g = jax.nn.sigmoid(xg @ g1) @ g2
    hs = lambda t: t.astype(F32).reshape(B, T, H_A, HEAD_A)
    r, k, v, decay, a = hs(r), hs(k), hs(v), hs(decay), hs(a)
    kk = k * k_k.astype(F32).reshape(H_A, HEAD_A)
    kk = kk / jnp.maximum(jnp.sqrt(jnp.sum(kk * kk, axis=-1, keepdims=True)), 1e-12)
    k = k * (1.0 + (a - 1.0) * k_a.astype(F32).reshape(H_A, HEAD_A))
    y, S = wkv7_scan(r, decay, k, v, -kk, kk * a, S0.astype(F32))
    mean = jnp.mean(y, axis=-1, keepdims=True)
    var = jnp.mean(jnp.square(y - mean), axis=-1, keepdims=True)
    yn = (y - mean) * lax.rsqrt(var + GN_EPS)
    yn = yn * lnx_g.astype(F32).reshape(H_A, HEAD_A) + lnx_b.astype(F32).reshape(H_A, HEAD_A)
    yn = yn + jnp.sum(r * k * r_k.astype(F32), axis=-1, keepdims=True) * v
    out = (yn.reshape(B, T, D).astype(x.dtype) * g) @ w_o
    return out, x[:, -1, :], S.astype(S0.dtype)


def shared_kv(h, kv_g, w_kvf, b_f):
    B, T, _ = h.shape
    proj = rmsnorm(h, kv_g) @ w_kvf
    k = proj[..., :D_B].reshape(B, T, H_B, HEAD_B)
    v = proj[..., D_B:2 * D_B].reshape(B, T, H_B, HEAD_B)
    logf = jax.nn.log_sigmoid((proj[..., 2 * D_B:] + b_f).astype(F32))
    return k, v, logf


def fox_attention(xn, k_all, v_all, c_all, P, w_q, w_o):
    B, T, _ = xn.shape
    q = (xn @ w_q).reshape(B, T, H_B, HEAD_B)
    outs = []
    for qs in range(0, T, QBLK):
        qe = min(qs + QBLK, T)
        n_k = P + qe
        s = jnp.einsum('bqhd,bkhd->bhqk', q[:, qs:qe], k_all[:, :n_k]).astype(F32) * SCALE
        cq = jnp.transpose(c_all[:, P + qs:P + qe], (0, 2, 1))
        ck = jnp.transpose(c_all[:, :n_k], (0, 2, 1))
        s = s + cq[..., None] - ck[:, :, None, :]
        q_pos = P + qs + jnp.arange(qe - qs)
        k_pos = jnp.arange(n_k)
        s = jnp.where(k_pos[None, :] <= q_pos[:, None], s, -jnp.inf)
        p = jax.nn.softmax(s, axis=-1)
        outs.append(jnp.einsum('bhqk,bkhd->bqhd', p.astype(v_all.dtype), v_all[:, :n_k]))
    o = jnp.concatenate(outs, axis=1).reshape(B, T, D_B)
    return o @ w_o


def trunk(x, shift0, wkv0, past_k, past_v, past_logf,
          ln1_g, ln2_g, w_gate, w_up, w_down, mu, w_r, w_k, w_v, w_o, w0, w1, w2,
          a0, a1, a2, g1, g2, k_k, k_a, r_k, lnx_g, lnx_b, kv_g, w_kvf, b_f, w_q, w_ob, final_g):
    P = past_k.shape[1]
    h = x
    new_shift = []
    new_wkv = []
    for l in range(N_A):
        out, sh, S = rwkv7_time_mix(rmsnorm(h, ln1_g[l]), shift0[l], wkv0[l], mu[l], w_r[l], w_k[l],
                                    w_v[l], w_o[l], w0[l], w1[l], w2[l], a0[l], a1[l], a2[l],
                                    g1[l], g2[l], k_k[l], k_a[l], r_k[l], lnx_g[l], lnx_b[l])
        h = h + out
        h = h + swiglu(rmsnorm(h, ln2_g[l]), w_gate[l], w_up[l], w_down[l])
        new_shift.append(sh)
        new_wkv.append(S)
    k_new, v_new, logf_new = shared_kv(h, kv_g, w_kvf, b_f)
    k_all = jnp.concatenate([past_k.astype(k_new.dtype), k_new], axis=1)
    v_all = jnp.concatenate([past_v.astype(v_new.dtype), v_new], axis=1)
    c_all = jnp.cumsum(jnp.concatenate([past_logf.astype(F32), logf_new], axis=1), axis=1)
    for j in range(N_B):
        l = N_A + j
        h = h + fox_attention(rmsnorm(h, ln1_g[l]), k_all, v_all, c_all, P, w_q[j], w_ob[j])
        h = h + swiglu(rmsnorm(h, ln2_g[l]), w_gate[l], w_up[l], w_down[l])
    y = rmsnorm(h, final_g)
    return y, jnp.stack(new_shift), jnp.stack(new_wkv), k_new, v_new, logf_new.astype(x.dtype)


def setup_inputs(seed: int = 0) -> dict:
    key = jax.random.key(seed)
    ks = iter(jax.random.split(key, 48))
    nrm = lambda shape, scale: scale * jax.random.normal(next(ks), shape, jnp.float32)
    uni = lambda shape, lo, hi: jax.random.uniform(next(ks), shape, jnp.float32, lo, hi)
    D = D_MODEL
    return {
        'x_prompt': nrm((BATCH, SEQ, D), 1.0),
        'x_sample': nrm((DEC_BATCH, DEC_SEQ, D), 1.0),
        'state_shift': nrm((N_A, DEC_BATCH, D), 1.0),
        'state_wkv': nrm((N_A, DEC_BATCH, H_A, HEAD_A, HEAD_A), 0.5),
        'cache_k': nrm((DEC_BATCH, PAST_LEN, H_B, HEAD_B), 1.0),
        'cache_v': nrm((DEC_BATCH, PAST_LEN, H_B, HEAD_B), 1.0),
        'cache_logf': jax.nn.log_sigmoid(uni((DEC_BATCH, PAST_LEN, H_B), 1.0, 4.0)),
        'ln1_g': 1.0 + nrm((DEPTH, D), 0.02),
        'ln2_g': 1.0 + nrm((DEPTH, D), 0.02),
        'w_gate': nrm((DEPTH, D, D_FF), D ** -0.5),
        'w_up': nrm((DEPTH, D, D_FF), D ** -0.5),
        'w_down': nrm((DEPTH, D_FF, D), D_FF ** -0.5),
        'mu': uni((N_A, 6, D), 0.0, 1.0),
        'w_r': nrm((N_A, D, D), D ** -0.5),
        'w_k': nrm((N_A, D, D), D ** -0.5),
        'w_v': nrm((N_A, D, D), D ** -0.5),
        'w_o': nrm((N_A, D, D), D ** -0.5),
        'w0': uni((N_A, D), -5.0, 1.0),
        'w1': nrm((N_A, D, LORA_DECAY), D ** -0.5),
        'w2': nrm((N_A, LORA_DECAY, D), 0.1 * LORA_DECAY ** -0.5),
        'a0': nrm((N_A, D), 0.5),
        'a1': nrm((N_A, D, LORA_AAA), D ** -0.5),
        'a2': nrm((N_A, LORA_AAA, D), 0.1 * LORA_AAA ** -0.5),
        'g1': nrm((N_A, D, LORA_GATE), D ** -0.5),
        'g2': nrm((N_A, LORA_GATE, D), LORA_GATE ** -0.5),
        'k_k': 0.85 + nrm((N_A, D), 0.02),
        'k_a': 1.0 + nrm((N_A, D), 0.02),
        'r_k': nrm((N_A, H_A, HEAD_A), 0.1),
        'lnx_g': 1.0 + nrm((N_A, D), 0.02),
        'lnx_b': nrm((N_A, D), 0.02),
        'kv_g': 1.0 + nrm((D,), 0.02),
        'w_kvf': nrm((D, 2 * D_B + H_B), D ** -0.5),
        'b_f': uni((H_B,), 1.0, 4.0),
        'w_q': nrm((N_B, D, D_B), D ** -0.5),
        'w_ob': nrm((N_B, D_B, D), D_B ** -0.5),
        'final_g': 1.0 + nrm((D,), 0.02),
    }


def reference(x_prompt, x_sample, state_shift, state_wkv, cache_k, cache_v, cache_logf,
              ln1_g, ln2_g, w_gate, w_up, w_down, mu, w_r, w_k, w_v, w_o, w0, w1, w2,
              a0, a1, a2, g1, g2, k_k, k_a, r_k, lnx_g, lnx_b, kv_g, w_kvf, b_f, w_q, w_ob, final_g):
    weights = (ln1_g, ln2_g, w_gate, w_up, w_down, mu, w_r, w_k, w_v, w_o, w0, w1, w2,
               a0, a1, a2, g1, g2, k_k, k_a, r_k, lnx_g, lnx_b, kv_g, w_kvf, b_f, w_q, w_ob, final_g)
    B = x_prompt.shape[0]
    dt = x_prompt.dtype
    y_p, shift_p, wkv_p, k_p, v_p, lf_p = trunk(
        x_prompt,
        jnp.zeros((N_A, B, D_MODEL), dt),
        jnp.zeros((N_A, B, H_A, HEAD_A, HEAD_A), dt),
        jnp.zeros((B, 0, H_B, HEAD_B), dt),
        jnp.zeros((B, 0, H_B, HEAD_B), dt),
        jnp.zeros((B, 0, H_B), dt),
        *weights)
    y_s, shift_s, wkv_s, k_s, v_s, lf_s = trunk(
        x_sample, state_shift, state_wkv, cache_k, cache_v, cache_logf, *weights)
    return (y_p, y_s, shift_p, wkv_p, k_p, v_p, lf_p, shift_s, wkv_s, k_s, v_s, lf_s)
```

```python
import functools

import jax
import jax.numpy as jnp
from jax import lax
from jax.experimental import pallas as pl
from jax.experimental.pallas import tpu as pltpu

F32 = jnp.float32
BF16 = jnp.bfloat16

HEAD = 64
LANE_GROUP = 256
HEADS_PER_GROUP = LANE_GROUP // HEAD
CHUNK = 64
RMS_EPS = 1e-5
GN_EPS = 64e-5
VMEM_LIMIT = 56 * 1024 * 1024


def _cparams(sem):
    return pltpu.CompilerParams(dimension_semantics=sem, vmem_limit_bytes=VMEM_LIMIT)


def _dot(a, b):
    return jnp.dot(a, b, preferred_element_type=F32)


def _dot_nt(a, b):
    return lax.dot_general(a, b, (((1,), (1,)), ((), ())), preferred_element_type=F32)


def _dot_tn(a, b):
    return lax.dot_general(a, b, (((0,), (0,)), ((), ())), preferred_element_type=F32)


def _split(x, parts):
    out = []
    rem = x
    for _ in range(parts):
        p = rem.astype(BF16)
        out.append(p)
        rem = rem - p.astype(F32)
    return out


def _mm(dot, a, b, pa=1, pb=1):
    a_parts = _split(a, pa)
    b_parts = _split(b, pb)
    acc = None
    for i, ap in enumerate(a_parts):
        for j, bp in enumerate(b_parts):
            if i + j >= max(pa, pb):
                continue
            t = dot(ap, bp)
            acc = t if acc is None else acc + t
    return acc


def _rms(x, g):
    ms = jnp.mean(x * x, axis=-1, keepdims=True)
    return x * lax.rsqrt(ms + RMS_EPS) * g


def _sigmoid(z):
    return 1.0 / (1.0 + jnp.exp(-z))


def _softplus(z):
    return jnp.maximum(z, 0.0) + jnp.log(1.0 + jnp.exp(-jnp.abs(z)))


def _group_ones():
    r = lax.broadcasted_iota(jnp.int32, (LANE_GROUP, LANE_GROUP), 0) // HEAD
    c = lax.broadcasted_iota(jnp.int32, (LANE_GROUP, LANE_GROUP), 1) // HEAD
    return jnp.where(r == c, 1.0, 0.0).astype(BF16)


def _head_sum(x, ones_bd):
    outs = []
    for g in range(x.shape[1] // LANE_GROUP):
        xs = x[:, g * LANE_GROUP:(g + 1) * LANE_GROUP]
        hi, lo = _split(xs, 2)
        outs.append(_dot(hi, ones_bd) + _dot(lo, ones_bd))
    return jnp.concatenate(outs, axis=-1)


def _row_tile(n_rows, seq, cap):
    t = min(cap, seq)
    while seq % t:
        t //= 2
    assert t % 8 == 0 and n_rows % t == 0
    return t


def _full(shape):
    return pl.BlockSpec(shape, lambda *_: (0,) * len(shape))


def _rwkv_pre_kernel(tiles_per_seq, x_ref, xp_ref, sh_ref, ln_ref, mu_ref, wr_ref, wk_ref, wv_ref,
                     w1_ref, w2_ref, a1_ref, a2_ref, g1_ref, g2_ref, w0_ref, a0_ref, kk_ref, ka_ref,
                     r_out, lw_out, k_out, v_out, av_out, bv_out, g_out, xl_out):
    i = pl.program_id(0)
    ln = ln_ref[...]
    xn = _rms(x_ref[...], ln)
    tm = xn.shape[0]
    prev_tile_last = _rms(xp_ref[...], ln)[7:8, :]
    prev = jnp.where(i % tiles_per_seq == 0, sh_ref[...], prev_tile_last)
    row = lax.broadcasted_iota(jnp.int32, xn.shape, 0)
    x_prev = jnp.where(row == 0, prev, pltpu.roll(xn, 1, 0))
    xx = x_prev - xn

    def mix(j):
        return (xn + xx * mu_ref[j:j + 1, :]).astype(BF16)

    r = _dot(mix(0), wr_ref[...])
    k = _dot(mix(2), wk_ref[...])
    v = _dot(mix(3), wv_ref[...])
    wl = _dot(jnp.tanh(_dot(mix(1), w1_ref[...])).astype(BF16), w2_ref[...])
    al = _dot(_dot(mix(4), a1_ref[...]).astype(BF16), a2_ref[...])
    g = _dot(_sigmoid(_dot(mix(5), g1_ref[...])).astype(BF16), g2_ref[...])

    w_log = -_softplus(-(w0_ref[...] + wl)) - 0.5
    asig = _sigmoid(a0_ref[...] + al)
    kk = k * kk_ref[...]
    nrm = jnp.sqrt(_head_sum(kk * kk, _group_ones()))
    kk = kk / jnp.maximum(nrm, 1e-12)

    r_out[...] = r
    lw_out[...] = -jnp.exp(w_log)
    k_out[...] = k * (1.0 + (asig - 1.0) * ka_ref[...])
    v_out[...] = v
    av_out[...] = -kk
    bv_out[...] = kk * asig
    g_out[...] = g
    xl_out[...] = xn[tm - 1:tm, :]


def _rwkv_pre(x, shift0, seq, ln, mu, wr, wk, wv, w1, w2, a1, a2, g1, g2, w0, a0, k_k, k_a):
    n, d = x.shape
    b = n // seq
    tm = _row_tile(n, seq, 256)
    tps = seq // tm
    row_spec = pl.BlockSpec((tm, d), lambda i: (i, 0))
    vec = _full((1, d))
    outs = pl.pallas_call(
        functools.partial(_rwkv_pre_kernel, tps),
        grid=(n // tm,),
        in_specs=[
            row_spec,
            pl.BlockSpec((8, d), lambda i: (jnp.maximum(i * (tm // 8) - 1, 0), 0)),
            pl.BlockSpec((None, 1, d), lambda i: (i // tps, 0, 0)),
            vec, _full(mu.shape),
            _full(wr.shape), _full(wk.shape), _full(wv.shape),
            _full(w1.shape), _full(w2.shape), _full(a1.shape), _full(a2.shape),
            _full(g1.shape), _full(g2.shape),
            vec, vec, vec, vec,
        ],
        out_specs=[row_spec] * 7 + [pl.BlockSpec((None, 1, d), lambda i: (i // tps, 0, 0))],
        out_shape=[jax.ShapeDtypeStruct((n, d), F32)] * 7 + [jax.ShapeDtypeStruct((b, 1, d), F32)],
        compiler_params=_cparams(("arbitrary",)),
        name="rwkv_pre",
    )(x, x, shift0, ln, mu, wr, wk, wv, w1, w2, a1, a2, g1, g2, w0, a0, k_k, k_a)
    return outs


def _scan_kernel(r_ref, lw_ref, k_ref, v_ref, a_ref, b_ref, s0_ref, y_ref, so_ref, s_scr):
    c = pl.program_id(2)
    C = CHUNK
    G = LANE_GROUP
    ri = lax.broadcasted_iota(jnp.int32, (G, G), 0)
    ci = lax.broadcasted_iota(jnp.int32, (G, G), 1)
    bd = (ri // HEAD) == (ci // HEAD)

    def expand(x):
        return jnp.where(bd, jnp.concatenate([x] * HEADS_PER_GROUP, axis=0), 0.0)

    @pl.when(c == 0)
    def _():
        s_scr[...] = expand(s0_ref[...])

    tpos = lax.broadcasted_iota(jnp.int32, (C, G), 0)
    spos = lax.broadcasted_iota(jnp.int32, (C, G), 1) % C
    strict = spos < tpos
    incl = spos <= tpos
    tri = jnp.where(lax.broadcasted_iota(jnp.int32, (C, C), 0) >= lax.broadcasted_iota(jnp.int32, (C, C), 1),
                    1.0, 0.0).astype(BF16)

    r = r_ref[...]
    lw = lw_ref[...]
    k = k_ref[...]
    v = v_ref[...]
    a = a_ref[...]
    b = b_ref[...]

    cs = _mm(_dot, tri, lw, 1, 3)
    p_in = jnp.exp(cs)
    p_ex = jnp.exp(cs - lw)
    p_inv = jnp.exp(-cs)
    p_end = p_in[C - 1:C, :]
    rt = r * p_in
    at = a * p_ex
    bt = b * p_inv
    kt = k * p_inv

    S = s_scr[...]
    la = jnp.concatenate([at, rt], axis=0)
    ab = _mm(_dot_nt, la, expand(bt))
    ak = _mm(_dot_nt, la, expand(kt))
    n_ab = jnp.where(strict, ab[:C], 0.0)
    a_ak = jnp.where(strict, ak[:C], 0.0)
    a_rb = jnp.where(incl, ab[C:], 0.0)
    a_rk = jnp.where(incl, ak[C:], 0.0)
    vbd = expand(v)

    u = _mm(_dot_nt, at, S) + _mm(_dot, a_ak, vbd)
    npow = n_ab
    steps = C.bit_length() - 1
    for i in range(steps):
        u = u + _mm(_dot, npow, expand(u))
        if i + 1 < steps:
            npow = _mm(_dot, npow, expand(npow))

    y_ref[...] = _mm(_dot_nt, rt, S) + _mm(_dot, a_rb, expand(u)) + _mm(_dot, a_rk, vbd)
    s_new = S * p_end + jnp.where(bd, _mm(_dot_tn, u, bt * p_end) + _mm(_dot_tn, v, kt * p_end), 0.0)
    s_scr[...] = s_new

    @pl.when(c == pl.num_programs(2) - 1)
    def _():
        acc = s_new[0:HEAD]
        for h in range(1, HEADS_PER_GROUP):
            acc = acc + s_new[h * HEAD:(h + 1) * HEAD]
        so_ref[...] = acc


def _wkv_scan(r, lw, k, v, av, bv, s0c, seq):
    n, d = r.shape
    b = n // seq
    ng = d // LANE_GROUP
    nc = seq // CHUNK
    blk = pl.BlockSpec((CHUNK, LANE_GROUP), lambda bi, gi, ci: (bi * nc + ci, gi))
    st = pl.BlockSpec((None, None, HEAD, LANE_GROUP), lambda bi, gi, ci: (bi, gi, 0, 0))
    y, s_out = pl.pallas_call(
        _scan_kernel,
        grid=(b, ng, nc),
        in_specs=[blk] * 6 + [st],
        out_specs=[blk, st],
        out_shape=[jax.ShapeDtypeStruct((n, d), F32), jax.ShapeDtypeStruct(s0c.shape, F32)],
        scratch_shapes=[pltpu.VMEM((LANE_GROUP, LANE_GROUP), F32)],
        compiler_params=_cparams(("arbitrary", "arbitrary", "arbitrary")),
        name="wkv_scan",
    )(r, lw, k, v, av, bv, s0c)
    return y, s_out


def _rwkv_post_kernel(y_ref, r_ref, k_ref, v_ref, g_ref, h_ref, lg_ref, lb_ref, rk_ref, wo_ref, o_ref):
    ones_bd = _group_ones()
    y = y_ref[...]
    mean = _head_sum(y, ones_bd) * (1.0 / HEAD)
    yc = y - mean
    var = _head_sum(yc * yc, ones_bd) * (1.0 / HEAD)
    yn = yc * lax.rsqrt(var + GN_EPS) * lg_ref[...] + lb_ref[...]
    v = v_ref[...]
    yn = yn + _head_sum(r_ref[...] * k_ref[...] * rk_ref[...], ones_bd) * v
    o_ref[...] = h_ref[...] + _dot((yn * g_ref[...]).astype(BF16), wo_ref[...])


def _rwkv_post(y, r, k, v, g, h, lnx_g, lnx_b, r_k, wo):
    n, d = y.shape
    tm = _row_tile(n, n, 256)
    row = pl.BlockSpec((tm, d), lambda i: (i, 0))
    vec = _full((1, d))
    return pl.pallas_call(
        _rwkv_post_kernel,
        grid=(n // tm,),
        in_specs=[row] * 6 + [vec, vec, vec, _full(wo.shape)],
        out_specs=row,
        out_shape=jax.ShapeDtypeStruct((n, d), F32),
        compiler_params=_cparams(("arbitrary",)),
        name="rwkv_post",
    )(y, r, k, v, g, h, lnx_g, lnx_b, r_k, wo)


def _ffn_kernel(final, h_ref, ln_ref, wg_ref, wu_ref, wd_ref, *rest):
    if final:
        fg_ref, o_ref, y_ref, xn_scr, acc_scr = rest
    else:
        o_ref, xn_scr, acc_scr = rest
    j = pl.program_id(1)

    @pl.when(j == 0)
    def _():
        xn_scr[...] = _rms(h_ref[...], ln_ref[...]).astype(BF16)
        acc_scr[...] = jnp.zeros_like(acc_scr)

    xn = xn_scr[...]
    gate = _dot(xn, wg_ref[...])
    up = _dot(xn, wu_ref[...])
    act = (gate * _sigmoid(gate) * up).astype(BF16)
    acc_scr[...] += _dot(act, wd_ref[...])

    @pl.when(j == pl.num_programs(1) - 1)
    def _():
        out = h_ref[...] + acc_scr[...]
        o_ref[...] = out
        if final:
            y_ref[...] = _rms(out, fg_ref[...])


def _ffn(h, ln, wg, wu, wd, final_g=None):
    n, d = h.shape
    dff = wg.shape[1]
    tm = _row_tile(n, n, 1024)
    tf = 256
    assert dff % tf == 0
    final = final_g is not None
    row = pl.BlockSpec((tm, d), lambda i, j: (i, 0))
    vec = pl.BlockSpec((1, d), lambda i, j: (0, 0))
    in_specs = [row, vec,
                pl.BlockSpec((d, tf), lambda i, j: (0, j)),
                pl.BlockSpec((d, tf), lambda i, j: (0, j)),
                pl.BlockSpec((tf, d), lambda i, j: (j, 0))]
    args = [h, ln, wg, wu, wd]
    out_specs = [row]
    out_shape = [jax.ShapeDtypeStruct((n, d), F32)]
    if final:
        in_specs.append(vec)
        args.append(final_g)
        out_specs.append(row)
        out_shape.append(jax.ShapeDtypeStruct((n, d), F32))
    res = pl.pallas_call(
        functools.partial(_ffn_kernel, final),
        grid=(n // tm, dff // tf),
        in_specs=in_specs,
        out_specs=out_specs,
        out_shape=out_shape,
        scratch_shapes=[pltpu.VMEM((tm, d), BF16), pltpu.VMEM((tm, d), F32)],
        compiler_params=_cparams(("arbitrary", "arbitrary")),
        name="ffn_final" if final else "ffn",
    )(*args)
    return res


def _kvq_kernel(scale, h_ref, kvg_ref, ln_ref, wk_ref, wv_ref, wf_ref, bf_ref, wq_ref,
                k_out, v_out, f_out, q_out):
    h = h_ref[...]
    ms = jnp.mean(h * h, axis=-1, keepdims=True)
    hn = h * lax.rsqrt(ms + RMS_EPS)
    xkv = (hn * kvg_ref[...]).astype(BF16)
    xq = (hn * ln_ref[...]).astype(BF16)
    k_out[...] = _dot(xkv, wk_ref[...])
    v_out[...] = _dot(xkv, wv_ref[...])
    z = _dot(xkv, wf_ref[...]) + bf_ref[...]
    f_out[...] = -_softplus(-z)
    q_out[...] = (_dot(xq, wq_ref[...]) * scale).astype(BF16)


def _kvq(h, kv_g, ln, wk, wv, wf, b_f, wq, scale):
    n, d = h.shape
    nh = wf.shape[1]
    tm = _row_tile(n, n, 512)
    row = pl.BlockSpec((tm, d), lambda i: (i, 0))
    vec = _full((1, d))
    return pl.pallas_call(
        functools.partial(_kvq_kernel, scale),
        grid=(n // tm,),
        in_specs=[row, vec, vec, _full(wk.shape), _full(wv.shape), _full(wf.shape), _full((1, nh)), _full(wq.shape)],
        out_specs=[row, row, pl.BlockSpec((tm, nh), lambda i: (i, 0)), row],
        out_shape=[jax.ShapeDtypeStruct((n, d), F32), jax.ShapeDtypeStruct((n, d), F32),
                   jax.ShapeDtypeStruct((n, nh), F32), jax.ShapeDtypeStruct((n, d), BF16)],
        compiler_params=_cparams(("arbitrary",)),
        name="kvq_proj",
    )(h, kv_g, ln, wk, wv, wf, b_f, wq)


def _cumsum_kernel(blk, f_ref, c0_ref, col_ref, row_ref):
    t, nh = f_ref.shape
    tri = jnp.where(lax.broadcasted_iota(jnp.int32, (blk, blk), 0) >= lax.broadcasted_iota(jnp.int32, (blk, blk), 1),
                    1.0, 0.0).astype(BF16)
    eye = jnp.where(lax.broadcasted_iota(jnp.int32, (nh, nh), 0) == lax.broadcasted_iota(jnp.int32, (nh, nh), 1),
                    1.0, 0.0).astype(BF16)
    carry = c0_ref[...]
    for j in range(t // blk):
        cb = _mm(_dot, tri, f_ref[j * blk:(j + 1) * blk, :], 1, 3) + carry
        col_ref[j * blk:(j + 1) * blk, :] = cb
        row_ref[:, j * blk:(j + 1) * blk] = _mm(_dot_nt, eye, cb, 1, 3)
        carry = cb[blk - 1:blk, :]


def _cumsum(logf, c0):
    b, t, nh = logf.shape
    blk = min(t, 256)
    assert t % blk == 0
    return pl.pallas_call(
        functools.partial(_cumsum_kernel, blk),
        grid=(b,),
        in_specs=[pl.BlockSpec((None, t, nh), lambda i: (i, 0, 0)), pl.BlockSpec((None, 1, nh), lambda i: (i, 0, 0))],
        out_specs=[pl.BlockSpec((None, t, nh), lambda i: (i, 0, 0)), pl.BlockSpec((None, nh, t), lambda i: (i, 0, 0))],
        out_shape=[jax.ShapeDtypeStruct((b, t, nh), F32), jax.ShapeDtypeStruct((b, nh, t), F32)],
        compiler_params=_cparams(("arbitrary",)),
        name="logf_cumsum",
    )(logf, c0)


def _head_pair_masks(shape):
    lane = lax.broadcasted_iota(jnp.int32, shape, len(shape) - 1)
    return lane < HEAD


def _attn_prompt_kernel(tk, q_ref, k_ref, v_ref, cq_ref, ck_ref, o_ref):
    qi = pl.program_id(2)
    q = q_ref[...]
    tq = q.shape[0]
    first = _head_pair_masks(q.shape)
    zero = jnp.zeros_like(q)
    qh = (jnp.where(first, q, zero), jnp.where(first, zero, q))
    cq = (cq_ref[:, 0:1], cq_ref[:, 1:2])

    def block(kb, carry, masked):
        start = pl.multiple_of(kb * tk, tk)
        kblk = k_ref[pl.ds(start, tk), :].astype(BF16)
        vblk = v_ref[pl.ds(start, tk), :].astype(BF16)
        ck = ck_ref[kb]
        new = []
        for h in range(2):
            m, l, acc = carry[h]
            s = _dot_nt(qh[h], kblk) + cq[h] - ck[h:h + 1, :]
            if masked:
                qpos = lax.broadcasted_iota(jnp.int32, s.shape, 0)
                kpos = lax.broadcasted_iota(jnp.int32, s.shape, 1)
                s = jnp.where(kpos <= qpos, s, -jnp.inf)
            m_new = jnp.maximum(m, jnp.max(s, axis=-1, keepdims=True))
            p = jnp.exp(s - m_new)
            alpha = jnp.exp(m - m_new)
            l = alpha * l + jnp.sum(p, axis=-1, keepdims=True)
            acc = alpha * acc + _dot(p.astype(BF16), vblk)
            new.append((m_new, l, acc))
        return tuple(new)

    init = tuple((jnp.full((tq, 1), -jnp.inf, F32), jnp.zeros((tq, 1), F32), jnp.zeros((tq, 2 * HEAD), F32))
                 for _ in range(2))
    carry = lax.fori_loop(0, qi, lambda kb, cr: block(kb, cr, False), init)
    (_, l0, acc0), (_, l1, acc1) = block(qi, carry, True)
    o_ref[...] = jnp.where(first, acc0 / l0, acc1 / l1).astype(o_ref.dtype)


def _attn_prompt(q, k, v, c_col, c_row, seq):
    n, d = q.shape
    b = n // seq
    npair = d // (2 * HEAD)
    tq = min(seq, 256)
    nq = seq // tq
    cq = c_col.reshape(b, seq, npair, 2).transpose(0, 2, 1, 3)
    ck = c_row.reshape(b, npair, 2, nq, tq).transpose(0, 1, 3, 2, 4)
    return pl.pallas_call(
        functools.partial(_attn_prompt_kernel, tq),
        grid=(b, npair, nq),
        in_specs=[
            pl.BlockSpec((tq, 2 * HEAD), lambda bi, pi, qi: (bi * nq + qi, pi)),
            pl.BlockSpec((seq, 2 * HEAD), lambda bi, pi, qi: (bi, pi)),
            pl.BlockSpec((seq, 2 * HEAD), lambda bi, pi, qi: (bi, pi)),
            pl.BlockSpec((None, None, tq, 2), lambda bi, pi, qi: (bi, pi, qi, 0)),
            pl.BlockSpec((None, None, nq, 2, tq), lambda bi, pi, qi: (bi, pi, 0, 0, 0)),
        ],
        out_specs=pl.BlockSpec((tq, 2 * HEAD), lambda bi, pi, qi: (bi * nq + qi, pi)),
        out_shape=jax.ShapeDtypeStruct((n, d), BF16),
        compiler_params=_cparams(("arbitrary", "arbitrary", "arbitrary")),
        name="fox_attn_prompt",
    )(q, k, v, cq, ck)


def _attn_sample_kernel(q_ref, kp_ref, vp_ref, kn_ref, vn_ref, cq_ref, ckp_ref, ckn_ref, o_ref):
    q = q_ref[...]
    first = _head_pair_masks(q.shape)
    zero = jnp.zeros_like(q)
    qh = (jnp.where(first, q, zero), jnp.where(first, zero, q))
    kp = kp_ref[...].astype(BF16)
    vp = vp_ref[...].astype(BF16)
    kn = kn_ref[...].astype(BF16)
    vn = vn_ref[...].astype(BF16)
    outs = []
    for h in range(2):
        cq = cq_ref[:, h:h + 1]
        sp = _dot_nt(qh[h], kp) + cq - ckp_ref[h:h + 1, :]
        sn = _dot_nt(qh[h], kn) + cq - ckn_ref[h:h + 1, :]
        qpos = lax.broadcasted_iota(jnp.int32, sn.shape, 0)
        kpos = lax.broadcasted_iota(jnp.int32, sn.shape, 1)
        sn = jnp.where(kpos <= qpos, sn, -jnp.inf)
        m = jnp.maximum(jnp.max(sp, axis=-1, keepdims=True), jnp.max(sn, axis=-1, keepdims=True))
        pp = jnp.exp(sp - m)
        pn = jnp.exp(sn - m)
        l = jnp.sum(pp, axis=-1, keepdims=True) + jnp.sum(pn, axis=-1, keepdims=True)
        outs.append((_dot(pp.astype(BF16), vp) + _dot(pn.astype(BF16), vn)) / l)
    o_ref[...] = jnp.where(first, outs[0], outs[1]).astype(o_ref.dtype)


def _attn_sample(q, kp, vp, kn, vn, cq_col, ckp_row, ckn_row, seq, past):
    n, d = q.shape
    b = n // seq
    npair = d // (2 * HEAD)
    cq = cq_col.reshape(b, seq, npair, 2).transpose(0, 2, 1, 3)
    ckp = ckp_row.reshape(b, npair, 2, past)
    ckn = ckn_row.reshape(b, npair, 2, seq)
    new = pl.BlockSpec((seq, 2 * HEAD), lambda bi, pi: (bi, pi))
    old = pl.BlockSpec((past, 2 * HEAD), lambda bi, pi: (bi, pi))
    return pl.pallas_call(
        _attn_sample_kernel,
        grid=(b, npair),
        in_specs=[new, old, old, new, new,
                  pl.BlockSpec((None, None, seq, 2), lambda bi, pi: (bi, pi, 0, 0)),
                  pl.BlockSpec((None, None, 2, past), lambda bi, pi: (bi, pi, 0, 0)),
                  pl.BlockSpec((None, None, 2, seq), lambda bi, pi: (bi, pi, 0, 0))],
        out_specs=new,
        out_shape=jax.ShapeDtypeStruct((n, d), BF16),
        compiler_params=_cparams(("arbitrary", "arbitrary")),
        name="fox_attn_sample",
    )(q, kp, vp, kn, vn, cq, ckp, ckn)


def _oproj_kernel(o_ref, h_ref, w_ref, out_ref):
    out_ref[...] = h_ref[...] + _dot(o_ref[...], w_ref[...])


def _oproj(o, h, w):
    n, d = h.shape
    tm = _row_tile(n, n, 512)
    row = pl.BlockSpec((tm, d), lambda i: (i, 0))
    return pl.pallas_call(
        _oproj_kernel,
        grid=(n // tm,),
        in_specs=[row, row, _full(w.shape)],
        out_specs=row,
        out_shape=jax.ShapeDtypeStruct((n, d), F32),
        compiler_params=_cparams(("arbitrary",)),
        name="attn_oproj",
    )(o, h, w)


def _pack_state(s):
    b, h, v, k = s.shape
    return s.reshape(b, h // HEADS_PER_GROUP, HEADS_PER_GROUP, v, k).transpose(0, 1, 3, 2, 4).reshape(
        b, h // HEADS_PER_GROUP, v, HEADS_PER_GROUP * k)


def _unpack_state(sc):
    b, g, v, gk = sc.shape
    k = gk // HEADS_PER_GROUP
    return sc.reshape(b, g, v, HEADS_PER_GROUP, k).transpose(0, 1, 3, 2, 4).reshape(b, g * HEADS_PER_GROUP, v, k)


def _trunk(x, shift0, wkv0, past, w):
    b, t, d = x.shape
    n = b * t
    h = x.reshape(n, d)
    n_a = w["w_r"].shape[0]
    new_shift, new_wkv = [], []
    for l in range(n_a):
        r, lw, k, v, av, bv, g, xl = _rwkv_pre(
            h, shift0[l].reshape(b, 1, d), t, w["ln1_g"][l], w["mu"][l], w["w_r"][l], w["w_k"][l], w["w_v"][l],
            w["w1"][l], w["w2"][l], w["a1"][l], w["a2"][l], w["g1"][l], w["g2"][l],
            w["w0"][l], w["a0"][l], w["k_k"][l], w["k_a"][l])
        tp = -(-t // CHUNK) * CHUNK
        scan_in = (r, lw, k, v, av, bv)
        if tp != t:
            scan_in = tuple(jnp.pad(z.reshape(b, t, d), ((0, 0), (0, tp - t), (0, 0))).reshape(b * tp, d)
                            for z in scan_in)
        y, s_out = _wkv_scan(*scan_in, _pack_state(wkv0[l]), tp)
        if tp != t:
            y = y.reshape(b, tp, d)[:, :t].reshape(n, d)
        h = _rwkv_post(y, r, k, v, g, h, w["lnx_g"][l], w["lnx_b"][l], w["r_k"][l], w["w_o"][l])
        h = _ffn(h, w["ln2_g"][l], w["w_gate"][l], w["w_up"][l], w["w_down"][l])[0]
        new_shift.append(xl.reshape(b, d))
        new_wkv.append(_unpack_state(s_out))

    nh = d // HEAD
    n_b = w["w_q"].shape[0]
    scale = HEAD ** -0.5
    k_new, v_new, logf, q = _kvq(h, w["kv_g"], w["ln1_g"][n_a], w["w_kvf_k"], w["w_kvf_v"], w["w_kvf_f"],
                                 w["b_f"], w["w_q"][0], scale)
    lf3 = logf.reshape(b, t, nh)
    zero_carry = jnp.zeros((b, 1, nh), F32)
    if past is None:
        c_col, c_row = _cumsum(lf3, zero_carry)
    else:
        pk, pv, plf = past
        p = pk.shape[1]
        cp_col, cp_row = _cumsum(plf.astype(F32), zero_carry)
        c_col, c_row = _cumsum(lf3, cp_col[:, p - 1:p, :])
    for j in range(n_b):
        l = n_a + j
        if j > 0:
            q = _q_only(h, w["ln1_g"][l], w["w_q"][j], scale)
        if past is None:
            o = _attn_prompt(q, k_new, v_new, c_col, c_row, t)
        else:
            o = _attn_sample(q, pk.reshape(b * p, d), pv.reshape(b * p, d), k_new, v_new,
                             c_col, cp_row, c_row, t, p)
        h = _oproj(o, h, w["w_ob"][j])
        last = j == n_b - 1
        res = _ffn(h, w["ln2_g"][l], w["w_gate"][l], w["w_up"][l], w["w_down"][l],
                   w["final_g"] if last else None)
        h = res[0]
        if last:
            y_out = res[1]
    return (y_out.reshape(b, t, d), jnp.stack(new_shift), jnp.stack(new_wkv),
            k_new.reshape(b, t, nh, HEAD), v_new.reshape(b, t, nh, HEAD), logf.reshape(b, t, nh))


def _q_only_kernel(scale, h_ref, ln_ref, wq_ref, q_out):
    q_out[...] = (_dot(_rms(h_ref[...], ln_ref[...]).astype(BF16), wq_ref[...]) * scale).astype(BF16)


def _q_only(h, ln, wq, scale):
    n, d = h.shape
    tm = _row_tile(n, n, 512)
    row = pl.BlockSpec((tm, d), lambda i: (i, 0))
    return pl.pallas_call(
        functools.partial(_q_only_kernel, scale),
        grid=(n // tm,),
        in_specs=[row, _full((1, d)), _full(wq.shape)],
        out_specs=row,
        out_shape=jax.ShapeDtypeStruct((n, d), BF16),
        compiler_params=_cparams(("arbitrary",)),
        name="q_proj",
    )(h, ln, wq)


def kernel(x_prompt, x_sample, state_shift, state_wkv, cache_k, cache_v, cache_logf, ln1_g, ln2_g, w_gate, w_up, w_down, mu, w_r, w_k, w_v, w_o, w0, w1, w2, a0, a1, a2, g1, g2, k_k, k_a, r_k, lnx_g, lnx_b, kv_g, w_kvf, b_f, w_q, w_ob, final_g):
    d = x_prompt.shape[-1]
    nh = d // HEAD
    n_a = w_r.shape[0]
    bf = lambda z: z.astype(BF16)
    vec = lambda z: z.reshape(z.shape[:-1] + (1, d)).astype(F32)
    w = dict(
        ln1_g=vec(ln1_g), ln2_g=vec(ln2_g), w_gate=bf(w_gate), w_up=bf(w_up), w_down=bf(w_down),
        mu=mu.astype(F32), w_r=bf(w_r), w_k=bf(w_k), w_v=bf(w_v), w_o=bf(w_o),
        w0=vec(w0), w1=bf(w1), w2=bf(w2), a0=vec(a0), a1=bf(a1), a2=bf(a2), g1=bf(g1), g2=bf(g2),
        k_k=vec(k_k), k_a=vec(k_a),
        r_k=r_k.reshape(n_a, 1, d).astype(F32),
        lnx_g=vec(lnx_g), lnx_b=vec(lnx_b), kv_g=vec(kv_g),
        w_kvf_k=bf(w_kvf[:, :d]), w_kvf_v=bf(w_kvf[:, d:2 * d]), w_kvf_f=bf(w_kvf[:, 2 * d:]),
        b_f=b_f.reshape(1, nh).astype(F32), w_q=bf(w_q), w_ob=bf(w_ob), final_g=vec(final_g),
    )
    bp = x_prompt.shape[0]
    dt = x_prompt.dtype
    y_p, shift_p, wkv_p, k_p, v_p, lf_p = _trunk(
        x_prompt, jnp.zeros((n_a, bp, d), dt), jnp.zeros((n_a, bp, nh, HEAD, HEAD), dt), None, w)
    y_s, shift_s, wkv_s, k_s, v_s, lf_s = _trunk(
        x_sample, state_shift, state_wkv, (cache_k, cache_v, cache_logf), w)
    return (y_p, y_s, shift_p, wkv_p, k_p, v_p, lf_p, shift_s, wkv_s, k_s, v_s, lf_s)
```

```python
import functools

import jax
import jax.numpy as jnp
from jax import lax
from jax.experimental import pallas as pl
from jax.experimental.pallas import tpu as pltpu

F32 = jnp.float32
BF16 = jnp.bfloat16

HEAD = 64
LANE_GROUP = 256
HEADS_PER_GROUP = LANE_GROUP // HEAD
CHUNK = 64
RMS_EPS = 1e-5
GN_EPS = 64e-5
LOG2E = 1.4426950408889634
AUG_LANES = 128
AUG_PIECES = 3
ONES_ROWS = 16
VMEM_LIMIT = 56 * 1024 * 1024


def _cparams(sem):
    return pltpu.CompilerParams(dimension_semantics=sem, vmem_limit_bytes=VMEM_LIMIT)


def _dot(a, b):
    return jnp.dot(a, b, preferred_element_type=F32)


def _dot_nt(a, b):
    return lax.dot_general(a, b, (((1,), (1,)), ((), ())), preferred_element_type=F32)


def _dot_tn(a, b):
    return lax.dot_general(a, b, (((0,), (0,)), ((), ())), preferred_element_type=F32)


def _split(x, parts):
    out = []
    rem = x
    for _ in range(parts):
        p = rem.astype(BF16)
        out.append(p)
        rem = rem - p.astype(F32)
    return out


def _mm(dot, a, b, pa=1, pb=1):
    a_parts = _split(a, pa)
    b_parts = _split(b, pb)
    acc = None
    for i, ap in enumerate(a_parts):
        for j, bp in enumerate(b_parts):
            if i + j >= max(pa, pb):
                continue
            t = dot(ap, bp)
            acc = t if acc is None else acc + t
    return acc


def _rms(x, g):
    ms = jnp.mean(x * x, axis=-1, keepdims=True)
    return x * lax.rsqrt(ms + RMS_EPS) * g


def _sigmoid(z):
    return 1.0 / (1.0 + jnp.exp(-z))


def _softplus(z):
    return jnp.maximum(z, 0.0) + jnp.log(1.0 + jnp.exp(-jnp.abs(z)))


def _group_ones():
    r = lax.broadcasted_iota(jnp.int32, (LANE_GROUP, LANE_GROUP), 0) // HEAD
    c = lax.broadcasted_iota(jnp.int32, (LANE_GROUP, LANE_GROUP), 1) // HEAD
    return jnp.where(r == c, 1.0, 0.0).astype(BF16)


def _head_sum(x, ones_bd):
    outs = []
    for g in range(x.shape[1] // LANE_GROUP):
        xs = x[:, g * LANE_GROUP:(g + 1) * LANE_GROUP]
        hi, lo = _split(xs, 2)
        outs.append(_dot(hi, ones_bd) + _dot(lo, ones_bd))
    return jnp.concatenate(outs, axis=-1)


def _row_tile(n_rows, seq, cap):
    t = min(cap, seq)
    while seq % t:
        t //= 2
    assert t % 8 == 0 and n_rows % t == 0
    return t


def _full(shape):
    return pl.BlockSpec(shape, lambda *_: (0,) * len(shape))


def _rwkv_pre_kernel(tiles_per_seq, x_ref, xp_ref, sh_ref, ln_ref, mu_ref, wr_ref, wk_ref, wv_ref,
                     w1_ref, w2_ref, a1_ref, a2_ref, g1_ref, g2_ref, w0_ref, a0_ref, kk_ref, ka_ref,
                     r_out, lw_out, k_out, v_out, av_out, bv_out, g_out, xl_out):
    i = pl.program_id(0)
    ln = ln_ref[...]
    xn = _rms(x_ref[...], ln)
    tm = xn.shape[0]
    prev_tile_last = _rms(xp_ref[...], ln)[7:8, :]
    prev = jnp.where(i % tiles_per_seq == 0, sh_ref[...], prev_tile_last)
    row = lax.broadcasted_iota(jnp.int32, xn.shape, 0)
    x_prev = jnp.where(row == 0, prev, pltpu.roll(xn, 1, 0))
    xx = x_prev - xn

    def mix(j):
        return (xn + xx * mu_ref[j:j + 1, :]).astype(BF16)

    r = _dot(mix(0), wr_ref[...])
    k = _dot(mix(2), wk_ref[...])
    v = _dot(mix(3), wv_ref[...])
    wl = _dot(jnp.tanh(_dot(mix(1), w1_ref[...])).astype(BF16), w2_ref[...])
    al = _dot(_dot(mix(4), a1_ref[...]).astype(BF16), a2_ref[...])
    g = _dot(_sigmoid(_dot(mix(5), g1_ref[...])).astype(BF16), g2_ref[...])

    w_log = -_softplus(-(w0_ref[...] + wl)) - 0.5
    asig = _sigmoid(a0_ref[...] + al)
    kk = k * kk_ref[...]
    nrm = jnp.sqrt(_head_sum(kk * kk, _group_ones()))
    kk = kk / jnp.maximum(nrm, 1e-12)

    r_out[...] = r
    lw_out[...] = -jnp.exp(w_log)
    k_out[...] = k * (1.0 + (asig - 1.0) * ka_ref[...])
    v_out[...] = v
    av_out[...] = -kk
    bv_out[...] = kk * asig
    g_out[...] = g
    xl_out[...] = xn[tm - 1:tm, :]


def _rwkv_pre(x, shift0, seq, ln, mu, wr, wk, wv, w1, w2, a1, a2, g1, g2, w0, a0, k_k, k_a):
    n, d = x.shape
    b = n // seq
    tm = _row_tile(n, seq, 256)
    tps = seq // tm
    row_spec = pl.BlockSpec((tm, d), lambda i: (i, 0))
    vec = _full((1, d))
    outs = pl.pallas_call(
        functools.partial(_rwkv_pre_kernel, tps),
        grid=(n // tm,),
        in_specs=[
            row_spec,
            pl.BlockSpec((8, d), lambda i: (jnp.maximum(i * (tm // 8) - 1, 0), 0)),
            pl.BlockSpec((None, 1, d), lambda i: (i // tps, 0, 0)),
            vec, _full(mu.shape),
            _full(wr.shape), _full(wk.shape), _full(wv.shape),
            _full(w1.shape), _full(w2.shape), _full(a1.shape), _full(a2.shape),
            _full(g1.shape), _full(g2.shape),
            vec, vec, vec, vec,
        ],
        out_specs=[row_spec] * 7 + [pl.BlockSpec((None, 1, d), lambda i: (i // tps, 0, 0))],
        out_shape=[jax.ShapeDtypeStruct((n, d), F32)] * 7 + [jax.ShapeDtypeStruct((b, 1, d), F32)],
        compiler_params=_cparams(("arbitrary",)),
        name="rwkv_pre",
    )(x, x, shift0, ln, mu, wr, wk, wv, w1, w2, a1, a2, g1, g2, w0, a0, k_k, k_a)
    return outs


def _scan_kernel(r_ref, lw_ref, k_ref, v_ref, a_ref, b_ref, s0_ref, y_ref, so_ref, s_scr):
    c = pl.program_id(1)
    C = CHUNK
    G = LANE_GROUP
    ri = lax.broadcasted_iota(jnp.int32, (G, G), 0)
    ci = lax.broadcasted_iota(jnp.int32, (G, G), 1)
    bd = (ri // HEAD) == (ci // HEAD)

    def expand(x):
        return jnp.where(bd, jnp.concatenate([x] * HEADS_PER_GROUP, axis=0), 0.0)

    n_groups = s_scr.shape[0]

    @pl.when(c == 0)
    def _():
        for g in range(n_groups):
            s_scr[g] = expand(s0_ref[g])

    tpos = lax.broadcasted_iota(jnp.int32, (C, G), 0)
    spos = lax.broadcasted_iota(jnp.int32, (C, G), 1) % C
    strict = spos < tpos
    incl = spos <= tpos
    tri = jnp.where(lax.broadcasted_iota(jnp.int32, (C, C), 0) >= lax.broadcasted_iota(jnp.int32, (C, C), 1),
                    1.0, 0.0).astype(BF16)
    steps = C.bit_length() - 1

    groups = range(n_groups)
    sls = [slice(g * G, (g + 1) * G) for g in groups]
    lw = [lw_ref[:, sl] for sl in sls]
    v = [v_ref[:, sl] for sl in sls]
    cs = [_mm(_dot, tri, lw[g], 1, 3) for g in groups]
    p_in = [jnp.exp(cs[g]) for g in groups]
    p_inv = [jnp.exp(-cs[g]) for g in groups]
    p_end = [p_in[g][C - 1:C, :] for g in groups]
    rt = [r_ref[:, sls[g]] * p_in[g] for g in groups]
    at = [a_ref[:, sls[g]] * jnp.exp(cs[g] - lw[g]) for g in groups]
    bt = [b_ref[:, sls[g]] * p_inv[g] for g in groups]
    kt = [k_ref[:, sls[g]] * p_inv[g] for g in groups]
    S = [s_scr[g] for g in groups]
    la = [jnp.concatenate([at[g], rt[g]], axis=0) for g in groups]
    ab = [_mm(_dot_nt, la[g], expand(bt[g])) for g in groups]
    ak = [_mm(_dot_nt, la[g], expand(kt[g])) for g in groups]
    npow = [jnp.where(strict, ab[g][:C], 0.0) for g in groups]
    a_ak = [jnp.where(strict, ak[g][:C], 0.0) for g in groups]
    a_rb = [jnp.where(incl, ab[g][C:], 0.0) for g in groups]
    a_rk = [jnp.where(incl, ak[g][C:], 0.0) for g in groups]
    vbd = [expand(v[g]) for g in groups]
    u = [_mm(_dot_nt, at[g], S[g]) + _mm(_dot, a_ak[g], vbd[g]) for g in groups]
    y0 = [_mm(_dot_nt, rt[g], S[g]) + _mm(_dot, a_rk[g], vbd[g]) for g in groups]
    for i in range(steps):
        u = [u[g] + _mm(_dot, npow[g], expand(u[g])) for g in groups]
        if i + 1 < steps:
            npow = [_mm(_dot, npow[g], expand(npow[g])) for g in groups]
    for g in groups:
        y_ref[:, sls[g]] = y0[g] + _mm(_dot, a_rb[g], expand(u[g]))
    for g in groups:
        upd = _mm(_dot_tn, u[g], bt[g] * p_end[g]) + _mm(_dot_tn, v[g], kt[g] * p_end[g])
        s_scr[g] = S[g] * p_end[g] + jnp.where(bd, upd, 0.0)

    @pl.when(c == pl.num_programs(1) - 1)
    def _():
        for g in range(n_groups):
            acc = s_scr[g, 0:HEAD, :]
            for h in range(1, HEADS_PER_GROUP):
                acc = acc + s_scr[g, h * HEAD:(h + 1) * HEAD, :]
            so_ref[g] = acc


def _wkv_scan(r, lw, k, v, av, bv, s0c, seq):
    n, d = r.shape
    b = n // seq
    ng = d // LANE_GROUP
    nc = seq // CHUNK
    blk = pl.BlockSpec((CHUNK, d), lambda bi, ci: (bi * nc + ci, 0))
    st = pl.BlockSpec((None, ng, HEAD, LANE_GROUP), lambda bi, ci: (bi, 0, 0, 0))
    y, s_out = pl.pallas_call(
        _scan_kernel,
        grid=(b, nc),
        in_specs=[blk] * 6 + [st],
        out_specs=[blk, st],
        out_shape=[jax.ShapeDtypeStruct((n, d), F32), jax.ShapeDtypeStruct(s0c.shape, F32)],
        scratch_shapes=[pltpu.VMEM((ng, LANE_GROUP, LANE_GROUP), F32)],
        compiler_params=_cparams(("arbitrary", "arbitrary")),
        name="wkv_scan",
    )(r, lw, k, v, av, bv, s0c)
    return y, s_out


def _rwkv_post_kernel(y_ref, r_ref, k_ref, v_ref, g_ref, h_ref, lg_ref, lb_ref, rk_ref, wo_ref, o_ref):
    ones_bd = _group_ones()
    y = y_ref[...]
    mean = _head_sum(y, ones_bd) * (1.0 / HEAD)
    yc = y - mean
    var = _head_sum(yc * yc, ones_bd) * (1.0 / HEAD)
    yn = yc * lax.rsqrt(var + GN_EPS) * lg_ref[...] + lb_ref[...]
    v = v_ref[...]
    yn = yn + _head_sum(r_ref[...] * k_ref[...] * rk_ref[...], ones_bd) * v
    o_ref[...] = h_ref[...] + _dot((yn * g_ref[...]).astype(BF16), wo_ref[...])


def _rwkv_post(y, r, k, v, g, h, lnx_g, lnx_b, r_k, wo):
    n, d = y.shape
    tm = _row_tile(n, n, 256)
    row = pl.BlockSpec((tm, d), lambda i: (i, 0))
    vec = _full((1, d))
    return pl.pallas_call(
        _rwkv_post_kernel,
        grid=(n // tm,),
        in_specs=[row] * 6 + [vec, vec, vec, _full(wo.shape)],
        out_specs=row,
        out_shape=jax.ShapeDtypeStruct((n, d), F32),
        compiler_params=_cparams(("arbitrary",)),
        name="rwkv_post",
    )(y, r, k, v, g, h, lnx_g, lnx_b, r_k, wo)


def _ffn_kernel(final, h_ref, ln_ref, wg_ref, wu_ref, wd_ref, *rest):
    if final:
        fg_ref, o_ref, y_ref, xn_scr, acc_scr = rest
    else:
        o_ref, xn_scr, acc_scr = rest
    j = pl.program_id(1)

    @pl.when(j == 0)
    def _():
        xn_scr[...] = _rms(h_ref[...], ln_ref[...]).astype(BF16)
        acc_scr[...] = jnp.zeros_like(acc_scr)

    xn = xn_scr[...]
    gate = _dot(xn, wg_ref[...])
    up = _dot(xn, wu_ref[...])
    act = (gate * _sigmoid(gate) * up).astype(BF16)
    acc_scr[...] += _dot(act, wd_ref[...])

    @pl.when(j == pl.num_programs(1) - 1)
    def _():
        out = h_ref[...] + acc_scr[...]
        o_ref[...] = out
        if final:
            y_ref[...] = _rms(out, fg_ref[...])


def _ffn(h, ln, wg, wu, wd, final_g=None):
    n, d = h.shape
    dff = wg.shape[1]
    tm = _row_tile(n, n, 1024)
    tf = 256
    assert dff % tf == 0
    final = final_g is not None
    row = pl.BlockSpec((tm, d), lambda i, j: (i, 0))
    vec = pl.BlockSpec((1, d), lambda i, j: (0, 0))
    in_specs = [row, vec,
                pl.BlockSpec((d, tf), lambda i, j: (0, j)),
                pl.BlockSpec((d, tf), lambda i, j: (0, j)),
                pl.BlockSpec((tf, d), lambda i, j: (j, 0))]
    args = [h, ln, wg, wu, wd]
    out_specs = [row]
    out_shape = [jax.ShapeDtypeStruct((n, d), F32)]
    if final:
        in_specs.append(vec)
        args.append(final_g)
        out_specs.append(row)
        out_shape.append(jax.ShapeDtypeStruct((n, d), F32))
    res = pl.pallas_call(
        functools.partial(_ffn_kernel, final),
        grid=(n // tm, dff // tf),
        in_specs=in_specs,
        out_specs=out_specs,
        out_shape=out_shape,
        scratch_shapes=[pltpu.VMEM((tm, d), BF16), pltpu.VMEM((tm, d), F32)],
        compiler_params=_cparams(("arbitrary", "arbitrary")),
        name="ffn_final" if final else "ffn",
    )(*args)
    return res


def _kvq_kernel(scale, h_ref, kvg_ref, ln_ref, wk_ref, wv_ref, wf_ref, bf_ref, wq_ref,
                k_out, v_out, f_out, q_out):
    h = h_ref[...]
    ms = jnp.mean(h * h, axis=-1, keepdims=True)
    hn = h * lax.rsqrt(ms + RMS_EPS)
    xkv = (hn * kvg_ref[...]).astype(BF16)
    xq = (hn * ln_ref[...]).astype(BF16)
    k_out[...] = _dot(xkv, wk_ref[...])
    v_out[...] = _dot(xkv, wv_ref[...])
    z = _dot(xkv, wf_ref[...]) + bf_ref[...]
    f_out[...] = -_softplus(-z)
    q_out[...] = (_dot(xq, wq_ref[...]) * scale).astype(BF16)


def _kvq(h, kv_g, ln, wk, wv, wf, b_f, wq, scale):
    n, d = h.shape
    nh = wf.shape[1]
    tm = _row_tile(n, n, 512)
    row = pl.BlockSpec((tm, d), lambda i: (i, 0))
    vec = _full((1, d))
    return pl.pallas_call(
        functools.partial(_kvq_kernel, scale),
        grid=(n // tm,),
        in_specs=[row, vec, vec, _full(wk.shape), _full(wv.shape), _full(wf.shape), _full((1, nh)), _full(wq.shape)],
        out_specs=[row, row, pl.BlockSpec((tm, nh), lambda i: (i, 0)), row],
        out_shape=[jax.ShapeDtypeStruct((n, d), F32), jax.ShapeDtypeStruct((n, d), F32),
                   jax.ShapeDtypeStruct((n, nh), F32), jax.ShapeDtypeStruct((n, d), BF16)],
        compiler_params=_cparams(("arbitrary",)),
        name="kvq_proj",
    )(h, kv_g, ln, wk, wv, wf, b_f, wq)


def _cumsum_kernel(blk, f_ref, c0_ref, row_ref, aug_ref, last_ref):
    t, nh = f_ref.shape
    tri = jnp.where(lax.broadcasted_iota(jnp.int32, (blk, blk), 0) >= lax.broadcasted_iota(jnp.int32, (blk, blk), 1),
                    1.0, 0.0).astype(BF16)
    eye = jnp.where(lax.broadcasted_iota(jnp.int32, (nh, nh), 0) == lax.broadcasted_iota(jnp.int32, (nh, nh), 1),
                    1.0, 0.0).astype(BF16)
    head = lax.broadcasted_iota(jnp.int32, (nh, AUG_LANES), 0)
    lane = lax.broadcasted_iota(jnp.int32, (nh, AUG_LANES), 1)
    place = [jnp.where(lane == AUG_PIECES * head + i, 1.0, 0.0).astype(BF16) for i in range(AUG_PIECES)]
    carry = c0_ref[...]
    for j in range(t // blk):
        cb = _mm(_dot, tri, f_ref[j * blk:(j + 1) * blk, :], 1, 3) + carry
        row_ref[:, j * blk:(j + 1) * blk] = _mm(_dot_nt, eye, cb, 1, 3)
        pieces = _split(cb * LOG2E, AUG_PIECES)
        aug = _dot(pieces[0], place[0])
        for i in range(1, AUG_PIECES):
            aug = aug + _dot(pieces[i], place[i])
        aug_ref[j * blk:(j + 1) * blk, :] = aug.astype(BF16)
        carry = cb[blk - 1:blk, :]
    last_ref[...] = carry


def _cumsum(logf, c0):
    b, t, nh = logf.shape
    blk = min(t, 256)
    assert t % blk == 0 and AUG_PIECES * nh <= AUG_LANES
    return pl.pallas_call(
        functools.partial(_cumsum_kernel, blk),
        grid=(b,),
        in_specs=[pl.BlockSpec((None, t, nh), lambda i: (i, 0, 0)), pl.BlockSpec((None, 1, nh), lambda i: (i, 0, 0))],
        out_specs=[pl.BlockSpec((None, nh, t), lambda i: (i, 0, 0)),
                   pl.BlockSpec((None, t, AUG_LANES), lambda i: (i, 0, 0)),
                   pl.BlockSpec((None, 1, nh), lambda i: (i, 0, 0))],
        out_shape=[jax.ShapeDtypeStruct((b, nh, t), F32), jax.ShapeDtypeStruct((b, t, AUG_LANES), BF16),
                   jax.ShapeDtypeStruct((b, 1, nh), F32)],
        compiler_params=_cparams(("arbitrary",)),
        name="logf_cumsum",
    )(logf, c0)


def _head_pair_masks(shape):
    lane = lax.broadcasted_iota(jnp.int32, shape, len(shape) - 1)
    return lane < HEAD


def _attn_prompt_kernel(tk, q_ref, k_ref, v_ref, aug_ref, o_ref, ka_scr, vt_scr, st_scr, pm_scr, pt_scr):
    pi = pl.program_id(1)
    seq = q_ref.shape[0]
    nk = seq // tk
    vrows = HEAD + ONES_ROWS

    ka_scr[:, 0:2 * HEAD] = k_ref[...].astype(BF16)
    ka_scr[:, 2 * HEAD:] = aug_ref[...]
    ones = jnp.ones((ONES_ROWS, tk), BF16)
    for kb in range(nk):
        vt = v_ref[kb * tk:(kb + 1) * tk, :].T
        for h in range(2):
            vt_scr[h, 0:HEAD, kb * tk:(kb + 1) * tk] = vt[h * HEAD:(h + 1) * HEAD].astype(BF16)
            vt_scr[h, HEAD:vrows, kb * tk:(kb + 1) * tk] = ones

    lane = lax.broadcasted_iota(jnp.int32, (tk, AUG_LANES), 1)
    firstq = _head_pair_masks((tk, 2 * HEAD))
    minus = []
    for h in range(2):
        lo = AUG_PIECES * (2 * pi + h)
        minus.append(jnp.where((lane >= lo) & (lane < lo + AUG_PIECES), -1.0, 0.0).astype(BF16))
    kpos = lax.broadcasted_iota(jnp.int32, (tk, tk), 0)
    qpos = lax.broadcasted_iota(jnp.int32, (tk, tk), 1)
    causal = kpos <= qpos

    def scores(j):
        slot = j % 2
        q = q_ref[j * tk:(j + 1) * tk, :]
        zero = jnp.zeros_like(q)
        for h in range(2):
            qh = jnp.where(firstq, q, zero) if h == 0 else jnp.where(firstq, zero, q)
            qa = jnp.concatenate([qh, minus[h]], axis=1)
            pm = None
            for kb in range(j + 1):
                s = _dot_nt(ka_scr[kb * tk:(kb + 1) * tk, :], qa)
                if kb == j:
                    s = jnp.where(causal, s, -jnp.inf)
                st_scr[slot, h, kb * tk:(kb + 1) * tk, :] = s
                part = jnp.max(s.reshape(tk // 8, 8, tk), axis=0)
                pm = part if pm is None else jnp.maximum(pm, part)
            pm_scr[slot, h] = pm

    def softmax_out(j):
        slot = j % 2
        outs = []
        for h in range(2):
            mx = jnp.max(pm_scr[slot, h], axis=0, keepdims=True)
            for kb in range(j + 1):
                rows = slice(kb * tk, (kb + 1) * tk)
                pt_scr[slot, h, rows, :] = jnp.exp2(st_scr[slot, h, rows, :] - mx).astype(BF16)
            acc = _dot(vt_scr[h, :, 0:(j + 1) * tk], pt_scr[slot, h, 0:(j + 1) * tk, :])
            outs.append(acc[0:HEAD] / acc[HEAD:HEAD + 1])
        o_ref[j * tk:(j + 1) * tk, :] = jnp.concatenate(outs, axis=0).T.astype(o_ref.dtype)

    scores(0)
    for j in range(nk):
        if j + 1 < nk:
            scores(j + 1)
        softmax_out(j)


def _attn_prompt(q, k, v, c_aug, seq):
    n, d = q.shape
    b = n // seq
    npair = d // (2 * HEAD)
    tk = min(seq, 256)
    assert seq % tk == 0
    blk = pl.BlockSpec((seq, 2 * HEAD), lambda bi, pi: (bi, pi))
    return pl.pallas_call(
        functools.partial(_attn_prompt_kernel, tk),
        grid=(b, npair),
        in_specs=[blk, blk, blk, pl.BlockSpec((None, seq, AUG_LANES), lambda bi, pi: (bi, 0, 0))],
        out_specs=blk,
        out_shape=jax.ShapeDtypeStruct((n, d), BF16),
        scratch_shapes=[pltpu.VMEM((seq, 2 * HEAD + AUG_LANES), BF16),
                        pltpu.VMEM((2, HEAD + ONES_ROWS, seq), BF16),
                        pltpu.VMEM((2, 2, seq, tk), F32),
                        pltpu.VMEM((2, 2, 8, tk), F32),
                        pltpu.VMEM((2, 2, seq, tk), BF16)],
        compiler_params=_cparams(("arbitrary", "arbitrary")),
        name="fox_attn_prompt",
    )(q, k, v, c_aug)


def _attn_sample_kernel(q_ref, kp_ref, vp_ref, kn_ref, vn_ref, ckp_ref, ckn_ref, o_ref):
    q = q_ref[...]
    first = _head_pair_masks(q.shape)
    zero = jnp.zeros_like(q)
    qh = (jnp.where(first, q, zero), jnp.where(first, zero, q))
    kp = kp_ref[...].astype(BF16)
    vp = vp_ref[...].astype(BF16)
    kn = kn_ref[...].astype(BF16)
    vn = vn_ref[...].astype(BF16)
    outs = []
    for h in range(2):
        sp = _dot_nt(qh[h], kp) - ckp_ref[h:h + 1, :] * LOG2E
        sn = _dot_nt(qh[h], kn) - ckn_ref[h:h + 1, :] * LOG2E
        qpos = lax.broadcasted_iota(jnp.int32, sn.shape, 0)
        kpos = lax.broadcasted_iota(jnp.int32, sn.shape, 1)
        sn = jnp.where(kpos <= qpos, sn, -jnp.inf)
        m = jnp.maximum(jnp.max(sp, axis=-1, keepdims=True), jnp.max(sn, axis=-1, keepdims=True))
        pp = jnp.exp2(sp - m)
        pn = jnp.exp2(sn - m)
        l = jnp.sum(pp, axis=-1, keepdims=True) + jnp.sum(pn, axis=-1, keepdims=True)
        outs.append((_dot(pp.astype(BF16), vp) + _dot(pn.astype(BF16), vn)) / l)
    o_ref[...] = jnp.where(first, outs[0], outs[1]).astype(o_ref.dtype)


def _attn_sample(q, kp, vp, kn, vn, ckp_row, ckn_row, seq, past):
    n, d = q.shape
    b = n // seq
    npair = d // (2 * HEAD)
    ckp = ckp_row.reshape(b, npair, 2, past)
    ckn = ckn_row.reshape(b, npair, 2, seq)
    new = pl.BlockSpec((seq, 2 * HEAD), lambda bi, pi: (bi, pi))
    old = pl.BlockSpec((past, 2 * HEAD), lambda bi, pi: (bi, pi))
    return pl.pallas_call(
        _attn_sample_kernel,
        grid=(b, npair),
        in_specs=[new, old, old, new, new,
                  pl.BlockSpec((None, None, 2, past), lambda bi, pi: (bi, pi, 0, 0)),
                  pl.BlockSpec((None, None, 2, seq), lambda bi, pi: (bi, pi, 0, 0))],
        out_specs=new,
        out_shape=jax.ShapeDtypeStruct((n, d), BF16),
        compiler_params=_cparams(("arbitrary", "arbitrary")),
        name="fox_attn_sample",
    )(q, kp, vp, kn, vn, ckp, ckn)


def _oproj_kernel(o_ref, h_ref, w_ref, out_ref):
    out_ref[...] = h_ref[...] + _dot(o_ref[...], w_ref[...])


def _oproj(o, h, w):
    n, d = h.shape
    tm = _row_tile(n, n, 512)
    row = pl.BlockSpec((tm, d), lambda i: (i, 0))
    return pl.pallas_call(
        _oproj_kernel,
        grid=(n // tm,),
        in_specs=[row, row, _full(w.shape)],
        out_specs=row,
        out_shape=jax.ShapeDtypeStruct((n, d), F32),
        compiler_params=_cparams(("arbitrary",)),
        name="attn_oproj",
    )(o, h, w)


def _pack_state(s):
    b, h, v, k = s.shape
    return s.reshape(b, h // HEADS_PER_GROUP, HEADS_PER_GROUP, v, k).transpose(0, 1, 3, 2, 4).reshape(
        b, h // HEADS_PER_GROUP, v, HEADS_PER_GROUP * k)


def _unpack_state(sc):
    b, g, v, gk = sc.shape
    k = gk // HEADS_PER_GROUP
    return sc.reshape(b, g, v, HEADS_PER_GROUP, k).transpose(0, 1, 3, 2, 4).reshape(b, g * HEADS_PER_GROUP, v, k)


def _trunk(x, shift0, wkv0, past, w):
    b, t, d = x.shape
    n = b * t
    h = x.reshape(n, d)
    n_a = w["w_r"].shape[0]
    new_shift, new_wkv = [], []
    for l in range(n_a):
        r, lw, k, v, av, bv, g, xl = _rwkv_pre(
            h, shift0[l].reshape(b, 1, d), t, w["ln1_g"][l], w["mu"][l], w["w_r"][l], w["w_k"][l], w["w_v"][l],
            w["w1"][l], w["w2"][l], w["a1"][l], w["a2"][l], w["g1"][l], w["g2"][l],
            w["w0"][l], w["a0"][l], w["k_k"][l], w["k_a"][l])
        tp = -(-t // CHUNK) * CHUNK
        scan_in = (r, lw, k, v, av, bv)
        if tp != t:
            scan_in = tuple(jnp.pad(z.reshape(b, t, d), ((0, 0), (0, tp - t), (0, 0))).reshape(b * tp, d)
                            for z in scan_in)
        y, s_out = _wkv_scan(*scan_in, _pack_state(wkv0[l]), tp)
        if tp != t:
            y = y.reshape(b, tp, d)[:, :t].reshape(n, d)
        h = _rwkv_post(y, r, k, v, g, h, w["lnx_g"][l], w["lnx_b"][l], w["r_k"][l], w["w_o"][l])
        h = _ffn(h, w["ln2_g"][l], w["w_gate"][l], w["w_up"][l], w["w_down"][l])[0]
        new_shift.append(xl.reshape(b, d))
        new_wkv.append(_unpack_state(s_out))

    nh = d // HEAD
    n_b = w["w_q"].shape[0]
    scale = HEAD ** -0.5 * LOG2E
    k_new, v_new, logf, q = _kvq(h, w["kv_g"], w["ln1_g"][n_a], w["w_kvf_k"], w["w_kvf_v"], w["w_kvf_f"],
                                 w["b_f"], w["w_q"][0], scale)
    lf3 = logf.reshape(b, t, nh)
    zero_carry = jnp.zeros((b, 1, nh), F32)
    if past is None:
        _, c_aug, _ = _cumsum(lf3, zero_carry)
    else:
        pk, pv, plf = past
        p = pk.shape[1]
        cp_row, _, cp_last = _cumsum(plf.astype(F32), zero_carry)
        c_row, _, _ = _cumsum(lf3, cp_last)
    for j in range(n_b):
        l = n_a + j
        if j > 0:
            q = _q_only(h, w["ln1_g"][l], w["w_q"][j], scale)
        if past is None:
            o = _attn_prompt(q, k_new, v_new, c_aug, t)
        else:
            o = _attn_sample(q, pk.reshape(b * p, d), pv.reshape(b * p, d), k_new, v_new,
                             cp_row, c_row, t, p)
        h = _oproj(o, h, w["w_ob"][j])
        last = j == n_b - 1
        res = _ffn(h, w["ln2_g"][l], w["w_gate"][l], w["w_up"][l], w["w_down"][l],
                   w["final_g"] if last else None)
        h = res[0]
        if last:
            y_out = res[1]
    return (y_out.reshape(b, t, d), jnp.stack(new_shift), jnp.stack(new_wkv),
            k_new.reshape(b, t, nh, HEAD), v_new.reshape(b, t, nh, HEAD), logf.reshape(b, t, nh))


def _q_only_kernel(scale, h_ref, ln_ref, wq_ref, q_out):
    q_out[...] = (_dot(_rms(h_ref[...], ln_ref[...]).astype(BF16), wq_ref[...]) * scale).astype(BF16)


def _q_only(h, ln, wq, scale):
    n, d = h.shape
    tm = _row_tile(n, n, 512)
    row = pl.BlockSpec((tm, d), lambda i: (i, 0))
    return pl.pallas_call(
        functools.partial(_q_only_kernel, scale),
        grid=(n // tm,),
        in_specs=[row, _full((1, d)), _full(wq.shape)],
        out_specs=row,
        out_shape=jax.ShapeDtypeStruct((n, d), BF16),
        compiler_params=_cparams(("arbitrary",)),
        name="q_proj",
    )(h, ln, wq)


def kernel(x_prompt, x_sample, state_shift, state_wkv, cache_k, cache_v, cache_logf, ln1_g, ln2_g, w_gate, w_up, w_down, mu, w_r, w_k, w_v, w_o, w0, w1, w2, a0, a1, a2, g1, g2, k_k, k_a, r_k, lnx_g, lnx_b, kv_g, w_kvf, b_f, w_q, w_ob, final_g):
    d = x_prompt.shape[-1]
    nh = d // HEAD
    n_a = w_r.shape[0]
    bf = lambda z: z.astype(BF16)
    vec = lambda z: z.reshape(z.shape[:-1] + (1, d)).astype(F32)
    w = dict(
        ln1_g=vec(ln1_g), ln2_g=vec(ln2_g), w_gate=bf(w_gate), w_up=bf(w_up), w_down=bf(w_down),
        mu=mu.astype(F32), w_r=bf(w_r), w_k=bf(w_k), w_v=bf(w_v), w_o=bf(w_o),
        w0=vec(w0), w1=bf(w1), w2=bf(w2), a0=vec(a0), a1=bf(a1), a2=bf(a2), g1=bf(g1), g2=bf(g2),
        k_k=vec(k_k), k_a=vec(k_a),
        r_k=r_k.reshape(n_a, 1, d).astype(F32),
        lnx_g=vec(lnx_g), lnx_b=vec(lnx_b), kv_g=vec(kv_g),
        w_kvf_k=bf(w_kvf[:, :d]), w_kvf_v=bf(w_kvf[:, d:2 * d]), w_kvf_f=bf(w_kvf[:, 2 * d:]),
        b_f=b_f.reshape(1, nh).astype(F32), w_q=bf(w_q), w_ob=bf(w_ob), final_g=vec(final_g),
    )
    bp = x_prompt.shape[0]
    dt = x_prompt.dtype
    y_p, shift_p, wkv_p, k_p, v_p, lf_p = _trunk(
        x_prompt, jnp.zeros((n_a, bp, d), dt), jnp.zeros((n_a, bp, nh, HEAD, HEAD), dt), None, w)
    y_s, shift_s, wkv_s, k_s, v_s, lf_s = _trunk(
        x_sample, state_shift, state_wkv, (cache_k, cache_v, cache_logf), w)
    return (y_p, y_s, shift_p, wkv_p, k_p, v_p, lf_p, shift_s, wkv_s, k_s, v_s, lf_s)
```

```python
import functools

import jax
import jax.numpy as jnp
from jax import lax
from jax.experimental import pallas as pl
from jax.experimental.pallas import tpu as pltpu

F32 = jnp.float32
BF16 = jnp.bfloat16

HEAD = 64
LANE_GROUP = 256
HEADS_PER_GROUP = LANE_GROUP // HEAD
CHUNK = 64
RMS_EPS = 1e-5
GN_EPS = 64e-5
LOG2E = 1.4426950408889634
AUG_LANES = 128
AUG_PIECES = 3
ONES_ROWS = 16
VMEM_LIMIT = 56 * 1024 * 1024


def _cparams(sem):
    return pltpu.CompilerParams(dimension_semantics=sem, vmem_limit_bytes=VMEM_LIMIT)


def _dot(a, b):
    return jnp.dot(a, b, preferred_element_type=F32)


def _dot_nt(a, b):
    return lax.dot_general(a, b, (((1,), (1,)), ((), ())), preferred_element_type=F32)


def _dot_tn(a, b):
    return lax.dot_general(a, b, (((0,), (0,)), ((), ())), preferred_element_type=F32)


def _split(x, parts):
    out = []
    rem = x
    for _ in range(parts):
        p = rem.astype(BF16)
        out.append(p)
        rem = rem - p.astype(F32)
    return out


def _mm(dot, a, b, pa=1, pb=1):
    a_parts = _split(a, pa)
    b_parts = _split(b, pb)
    acc = None
    for i, ap in enumerate(a_parts):
        for j, bp in enumerate(b_parts):
            if i + j >= max(pa, pb):
                continue
            t = dot(ap, bp)
            acc = t if acc is None else acc + t
    return acc


def _rms(x, g):
    ms = jnp.mean(x * x, axis=-1, keepdims=True)
    return x * lax.rsqrt(ms + RMS_EPS) * g


def _sigmoid(z):
    return 1.0 / (1.0 + jnp.exp(-z))


def _softplus(z):
    return jnp.maximum(z, 0.0) + jnp.log(1.0 + jnp.exp(-jnp.abs(z)))


def _group_ones():
    r = lax.broadcasted_iota(jnp.int32, (LANE_GROUP, LANE_GROUP), 0) // HEAD
    c = lax.broadcasted_iota(jnp.int32, (LANE_GROUP, LANE_GROUP), 1) // HEAD
    return jnp.where(r == c, 1.0, 0.0).astype(BF16)


def _head_sum(x, ones_bd):
    outs = []
    for g in range(x.shape[1] // LANE_GROUP):
        xs = x[:, g * LANE_GROUP:(g + 1) * LANE_GROUP]
        hi, lo = _split(xs, 2)
        outs.append(_dot(hi, ones_bd) + _dot(lo, ones_bd))
    return jnp.concatenate(outs, axis=-1)


def _row_tile(n_rows, seq, cap):
    t = min(cap, seq)
    while seq % t:
        t //= 2
    assert t % 8 == 0 and n_rows % t == 0
    return t


def _full(shape):
    return pl.BlockSpec(shape, lambda *_: (0,) * len(shape))


def _rwkv_pre_kernel(tiles_per_seq, x_ref, xp_ref, sh_ref, ln_ref, mu_ref, wr_ref, wk_ref, wv_ref,
                     w1_ref, w2_ref, a1_ref, a2_ref, g1_ref, g2_ref, w0_ref, a0_ref, kk_ref, ka_ref,
                     r_out, lw_out, k_out, v_out, av_out, bv_out, g_out, xl_out):
    i = pl.program_id(0)
    ln = ln_ref[...]
    xn = _rms(x_ref[...], ln)
    tm = xn.shape[0]
    prev_tile_last = _rms(xp_ref[...], ln)[7:8, :]
    prev = jnp.where(i % tiles_per_seq == 0, sh_ref[...], prev_tile_last)
    row = lax.broadcasted_iota(jnp.int32, xn.shape, 0)
    x_prev = jnp.where(row == 0, prev, pltpu.roll(xn, 1, 0))
    xx = x_prev - xn

    def mix(j):
        return (xn + xx * mu_ref[j:j + 1, :]).astype(BF16)

    r = _dot(mix(0), wr_ref[...])
    k = _dot(mix(2), wk_ref[...])
    v = _dot(mix(3), wv_ref[...])
    wl = _dot(jnp.tanh(_dot(mix(1), w1_ref[...])).astype(BF16), w2_ref[...])
    al = _dot(_dot(mix(4), a1_ref[...]).astype(BF16), a2_ref[...])
    g = _dot(_sigmoid(_dot(mix(5), g1_ref[...])).astype(BF16), g2_ref[...])

    w_log = -_softplus(-(w0_ref[...] + wl)) - 0.5
    asig = _sigmoid(a0_ref[...] + al)
    kk = k * kk_ref[...]
    nrm = jnp.sqrt(_head_sum(kk * kk, _group_ones()))
    kk = kk / jnp.maximum(nrm, 1e-12)

    r_out[...] = r
    lw_out[...] = -jnp.exp(w_log)
    k_out[...] = k * (1.0 + (asig - 1.0) * ka_ref[...])
    v_out[...] = v
    av_out[...] = -kk
    bv_out[...] = kk * asig
    g_out[...] = g
    xl_out[...] = xn[tm - 1:tm, :]


def _rwkv_pre(x, shift0, seq, ln, mu, wr, wk, wv, w1, w2, a1, a2, g1, g2, w0, a0, k_k, k_a):
    n, d = x.shape
    b = n // seq
    tm = _row_tile(n, seq, 256)
    tps = seq // tm
    row_spec = pl.BlockSpec((tm, d), lambda i: (i, 0))
    vec = _full((1, d))
    outs = pl.pallas_call(
        functools.partial(_rwkv_pre_kernel, tps),
        grid=(n // tm,),
        in_specs=[
            row_spec,
            pl.BlockSpec((8, d), lambda i: (jnp.maximum(i * (tm // 8) - 1, 0), 0)),
            pl.BlockSpec((None, 1, d), lambda i: (i // tps, 0, 0)),
            vec, _full(mu.shape),
            _full(wr.shape), _full(wk.shape), _full(wv.shape),
            _full(w1.shape), _full(w2.shape), _full(a1.shape), _full(a2.shape),
            _full(g1.shape), _full(g2.shape),
            vec, vec, vec, vec,
        ],
        out_specs=[row_spec] * 7 + [pl.BlockSpec((None, 1, d), lambda i: (i // tps, 0, 0))],
        out_shape=[jax.ShapeDtypeStruct((n, d), F32)] * 7 + [jax.ShapeDtypeStruct((b, 1, d), F32)],
        compiler_params=_cparams(("arbitrary",)),
        name="rwkv_pre",
    )(x, x, shift0, ln, mu, wr, wk, wv, w1, w2, a1, a2, g1, g2, w0, a0, k_k, k_a)
    return outs


def _scan_kernel(r_ref, lw_ref, k_ref, v_ref, a_ref, b_ref, s0_ref, y_ref, so_ref, s_scr):
    c = pl.program_id(1)
    C = CHUNK
    G = LANE_GROUP
    ri = lax.broadcasted_iota(jnp.int32, (G, G), 0)
    ci = lax.broadcasted_iota(jnp.int32, (G, G), 1)
    bd = (ri // HEAD) == (ci // HEAD)

    def expand(x):
        return jnp.where(bd, jnp.concatenate([x] * HEADS_PER_GROUP, axis=0), 0.0)

    n_seq, n_grp = s0_ref.shape[0], s0_ref.shape[1]
    chains = [(s, g) for s in range(n_seq) for g in range(n_grp)]
    n_groups = len(chains)

    @pl.when(c == 0)
    def _():
        for i, (s, g) in enumerate(chains):
            s_scr[i] = expand(s0_ref[s, g])

    tpos = lax.broadcasted_iota(jnp.int32, (C, G), 0)
    spos = lax.broadcasted_iota(jnp.int32, (C, G), 1) % C
    strict = spos < tpos
    incl = spos <= tpos
    tri = jnp.where(lax.broadcasted_iota(jnp.int32, (C, C), 0) >= lax.broadcasted_iota(jnp.int32, (C, C), 1),
                    1.0, 0.0).astype(BF16)
    steps = C.bit_length() - 1

    groups = range(n_groups)
    sls = [(s, slice(None), slice(g * G, (g + 1) * G)) for s, g in chains]
    lw = [lw_ref[sl] for sl in sls]
    v = [v_ref[sl] for sl in sls]
    cs = [_mm(_dot, tri, lw[g], 1, 3) for g in groups]
    p_in = [jnp.exp(cs[g]) for g in groups]
    p_inv = [jnp.exp(-cs[g]) for g in groups]
    p_end = [p_in[g][C - 1:C, :] for g in groups]
    rt = [r_ref[sls[g]] * p_in[g] for g in groups]
    at = [a_ref[sls[g]] * jnp.exp(cs[g] - lw[g]) for g in groups]
    bt = [b_ref[sls[g]] * p_inv[g] for g in groups]
    kt = [k_ref[sls[g]] * p_inv[g] for g in groups]
    S = [s_scr[g] for g in groups]
    la = [jnp.concatenate([at[g], rt[g]], axis=0) for g in groups]
    ab = [_mm(_dot_nt, la[g], expand(bt[g])) for g in groups]
    ak = [_mm(_dot_nt, la[g], expand(kt[g])) for g in groups]
    npow = [jnp.where(strict, ab[g][:C], 0.0) for g in groups]
    a_rb = [jnp.where(incl, ab[g][C:], 0.0) for g in groups]
    a_k = [jnp.where(jnp.concatenate([strict, incl], axis=0), ak[g], 0.0) for g in groups]
    from_v = [_mm(_dot, a_k[g], expand(v[g])) for g in groups]
    from_s = [_mm(_dot_nt, la[g], S[g]) for g in groups]
    w0 = [from_s[g][:C] + from_v[g][:C] for g in groups]
    y0 = [from_s[g][C:] + from_v[g][C:] for g in groups]
    eye = jnp.where(spos == tpos, 1.0, 0.0)
    tinv = [eye + npow[g] for g in groups]
    npow = [_mm(_dot, npow[g], expand(npow[g])) for g in groups]
    for i in range(1, steps):
        last = i + 1 == steps
        lhs = [tinv[g] if last else jnp.concatenate([tinv[g], npow[g]], axis=0) for g in groups]
        prod = [_mm(_dot, lhs[g], expand(npow[g])) for g in groups]
        tinv = [tinv[g] + prod[g][:C] for g in groups]
        if not last:
            npow = [prod[g][C:] for g in groups]
    u = [_mm(_dot, tinv[g], expand(w0[g])) for g in groups]
    for g in groups:
        y_ref[sls[g]] = y0[g] + _mm(_dot, a_rb[g], expand(u[g]))
    for g in groups:
        upd = _mm(_dot_tn, jnp.concatenate([u[g], v[g]], axis=0),
                  jnp.concatenate([bt[g], kt[g]], axis=0) * p_end[g])
        s_scr[g] = S[g] * p_end[g] + jnp.where(bd, upd, 0.0)

    @pl.when(c == pl.num_programs(1) - 1)
    def _():
        for i, (s, g) in enumerate(chains):
            acc = s_scr[i, 0:HEAD, :]
            for h in range(1, HEADS_PER_GROUP):
                acc = acc + s_scr[i, h * HEAD:(h + 1) * HEAD, :]
            so_ref[s, g] = acc


def _wkv_scan(r, lw, k, v, av, bv, s0c, seq):
    n, d = r.shape
    b = n // seq
    ng = d // LANE_GROUP
    nc = seq // CHUNK
    sps = 2 if b % 2 == 0 else 1
    blk = pl.BlockSpec((sps, CHUNK, d), lambda bi, ci: (bi, ci, 0))
    st = pl.BlockSpec((sps, ng, HEAD, LANE_GROUP), lambda bi, ci: (bi, 0, 0, 0))
    y, s_out = pl.pallas_call(
        _scan_kernel,
        grid=(b // sps, nc),
        in_specs=[blk] * 6 + [st],
        out_specs=[blk, st],
        out_shape=[jax.ShapeDtypeStruct((b, seq, d), F32), jax.ShapeDtypeStruct(s0c.shape, F32)],
        scratch_shapes=[pltpu.VMEM((sps * ng, LANE_GROUP, LANE_GROUP), F32)],
        compiler_params=_cparams(("arbitrary", "arbitrary")),
        name="wkv_scan",
    )(*(z.reshape(b, seq, d) for z in (r, lw, k, v, av, bv)), s0c)
    return y.reshape(n, d), s_out


def _rwkv_post_kernel(y_ref, r_ref, k_ref, v_ref, g_ref, h_ref, lg_ref, lb_ref, rk_ref, wo_ref, o_ref):
    ones_bd = _group_ones()
    y = y_ref[...]
    mean = _head_sum(y, ones_bd) * (1.0 / HEAD)
    yc = y - mean
    var = _head_sum(yc * yc, ones_bd) * (1.0 / HEAD)
    yn = yc * lax.rsqrt(var + GN_EPS) * lg_ref[...] + lb_ref[...]
    v = v_ref[...]
    yn = yn + _head_sum(r_ref[...] * k_ref[...] * rk_ref[...], ones_bd) * v
    o_ref[...] = h_ref[...] + _dot((yn * g_ref[...]).astype(BF16), wo_ref[...])


def _rwkv_post(y, r, k, v, g, h, lnx_g, lnx_b, r_k, wo):
    n, d = y.shape
    tm = _row_tile(n, n, 256)
    row = pl.BlockSpec((tm, d), lambda i: (i, 0))
    vec = _full((1, d))
    return pl.pallas_call(
        _rwkv_post_kernel,
        grid=(n // tm,),
        in_specs=[row] * 6 + [vec, vec, vec, _full(wo.shape)],
        out_specs=row,
        out_shape=jax.ShapeDtypeStruct((n, d), F32),
        compiler_params=_cparams(("arbitrary",)),
        name="rwkv_post",
    )(y, r, k, v, g, h, lnx_g, lnx_b, r_k, wo)


def _ffn_kernel(final, tf, h_ref, ln_ref, wg_ref, wu_ref, wd_ref, *rest):
    if final:
        fg_ref, o_ref, y_ref, act_scr = rest
    else:
        o_ref, act_scr = rest
    h = h_ref[...]
    xn = _rms(h, ln_ref[...]).astype(BF16)
    for c in range(act_scr.shape[1] // tf):
        cols = slice(c * tf, (c + 1) * tf)
        gate = _dot(xn, wg_ref[:, cols])
        up = _dot(xn, wu_ref[:, cols])
        act_scr[:, cols] = (gate * _sigmoid(gate) * up).astype(BF16)
    out = h + _dot(act_scr[...], wd_ref[...])
    o_ref[...] = out
    if final:
        y_ref[...] = _rms(out, fg_ref[...])


def _ffn(h, ln, wg, wu, wd, final_g=None):
    n, d = h.shape
    dff = wg.shape[1]
    tm = _row_tile(n, n, 1024)
    tf = 256
    assert dff % tf == 0
    final = final_g is not None
    row = pl.BlockSpec((tm, d), lambda i: (i, 0))
    vec = pl.BlockSpec((1, d), lambda i: (0, 0))
    once = pl.Buffered(1)
    in_specs = [row, vec,
                pl.BlockSpec((d, dff), lambda i: (0, 0), pipeline_mode=once),
                pl.BlockSpec((d, dff), lambda i: (0, 0), pipeline_mode=once),
                pl.BlockSpec((dff, d), lambda i: (0, 0), pipeline_mode=once)]
    args = [h, ln, wg, wu, wd]
    out_specs = [row]
    out_shape = [jax.ShapeDtypeStruct((n, d), F32)]
    if final:
        in_specs.append(vec)
        args.append(final_g)
        out_specs.append(row)
        out_shape.append(jax.ShapeDtypeStruct((n, d), F32))
    res = pl.pallas_call(
        functools.partial(_ffn_kernel, final, tf),
        grid=(n // tm,),
        in_specs=in_specs,
        out_specs=out_specs,
        out_shape=out_shape,
        scratch_shapes=[pltpu.VMEM((tm, dff), BF16)],
        compiler_params=_cparams(("arbitrary",)),
        name="ffn_final" if final else "ffn",
    )(*args)
    return res


def _kvq_kernel(scale, h_ref, kvg_ref, ln_ref, wkt_ref, wvt_ref, wft_ref, bf_ref, wq_ref,
                kt_out, vt_out, f_out, q_out):
    h = h_ref[...]
    ms = jnp.mean(h * h, axis=-1, keepdims=True)
    hn = h * lax.rsqrt(ms + RMS_EPS)
    xkv = (hn * kvg_ref[...]).astype(BF16)
    xq = (hn * ln_ref[...]).astype(BF16)
    kt_out[...] = _dot_nt(wkt_ref[...], xkv)
    vt_out[...] = _dot_nt(wvt_ref[...], xkv)
    z = _dot_nt(wft_ref[...], xkv) + bf_ref[...]
    f_out[...] = -_softplus(-z)
    q_out[...] = (_dot(xq, wq_ref[...]) * scale).astype(BF16)


def _kvq(h, seq, kv_g, ln, wkt, wvt, wft, b_f, wq, scale):
    n, d = h.shape
    b = n // seq
    nh = wft.shape[0]
    tm = _row_tile(n, seq, 512)
    tps = seq // tm
    row = pl.BlockSpec((tm, d), lambda i: (i, 0))
    vec = _full((1, d))
    col = pl.BlockSpec((None, d, tm), lambda i: (i // tps, 0, i % tps))
    return pl.pallas_call(
        functools.partial(_kvq_kernel, scale),
        grid=(n // tm,),
        in_specs=[row, vec, vec, _full(wkt.shape), _full(wvt.shape), _full(wft.shape), _full((nh, 1)), _full(wq.shape)],
        out_specs=[col, col, pl.BlockSpec((None, nh, tm), lambda i: (i // tps, 0, i % tps)), row],
        out_shape=[jax.ShapeDtypeStruct((b, d, seq), F32), jax.ShapeDtypeStruct((b, d, seq), F32),
                   jax.ShapeDtypeStruct((b, nh, seq), F32), jax.ShapeDtypeStruct((n, d), BF16)],
        compiler_params=_cparams(("arbitrary",)),
        name="kvq_proj",
    )(h, kv_g, ln, wkt, wvt, wft, b_f, wq)


def _cumsum_kernel(blk, f_ref, c0_ref, row_ref, aug_ref, last_ref):
    nh, t = f_ref.shape
    triu = jnp.where(lax.broadcasted_iota(jnp.int32, (blk, blk), 0) <= lax.broadcasted_iota(jnp.int32, (blk, blk), 1),
                     1.0, 0.0).astype(BF16)
    head = lax.broadcasted_iota(jnp.int32, (nh, AUG_LANES), 0)
    lane = lax.broadcasted_iota(jnp.int32, (nh, AUG_LANES), 1)
    place = [jnp.where(lane == AUG_PIECES * head + i, 1.0, 0.0).astype(BF16) for i in range(AUG_PIECES)]
    carry = c0_ref[...]
    for j in range(t // blk):
        cols = slice(j * blk, (j + 1) * blk)
        cb = _mm(_dot, f_ref[:, cols], triu, 3, 1) + carry
        row_ref[:, cols] = cb
        pieces = _split(cb * LOG2E, AUG_PIECES)
        aug = _dot_tn(pieces[0], place[0])
        for i in range(1, AUG_PIECES):
            aug = aug + _dot_tn(pieces[i], place[i])
        aug_ref[cols, :] = aug.astype(BF16)
        carry = cb[:, blk - 1:blk]
    last_ref[...] = carry


def _cumsum(logf, c0):
    b, nh, t = logf.shape
    blk = min(t, 256)
    assert t % blk == 0 and AUG_PIECES * nh <= AUG_LANES
    row = pl.BlockSpec((None, nh, t), lambda i: (i, 0, 0))
    one = pl.BlockSpec((None, nh, 1), lambda i: (i, 0, 0))
    return pl.pallas_call(
        functools.partial(_cumsum_kernel, blk),
        grid=(b,),
        in_specs=[row, one],
        out_specs=[row, pl.BlockSpec((None, t, AUG_LANES), lambda i: (i, 0, 0)), one],
        out_shape=[jax.ShapeDtypeStruct((b, nh, t), F32), jax.ShapeDtypeStruct((b, t, AUG_LANES), BF16),
                   jax.ShapeDtypeStruct((b, nh, 1), F32)],
        compiler_params=_cparams(("arbitrary",)),
        name="logf_cumsum",
    )(logf, c0)


def _head_pair_masks(shape):
    lane = lax.broadcasted_iota(jnp.int32, shape, len(shape) - 1)
    return lane < HEAD


def _attn_prompt_kernel(tk, q_ref, k_ref, v_ref, aug_ref, o_ref, ka_scr, vt_scr, st_scr, pm_scr, pt_scr):
    pi = pl.program_id(1)
    seq = q_ref.shape[0]
    nk = seq // tk
    vrows = HEAD + ONES_ROWS

    for kb in range(nk):
        ka_scr[kb * tk:(kb + 1) * tk, 0:2 * HEAD] = k_ref[:, kb * tk:(kb + 1) * tk].T.astype(BF16)
    ka_scr[:, 2 * HEAD:] = aug_ref[...]
    for h in range(2):
        vt_scr[h, 0:HEAD, :] = v_ref[h * HEAD:(h + 1) * HEAD, :].astype(BF16)
        vt_scr[h, HEAD:vrows, :] = jnp.ones((ONES_ROWS, seq), BF16)

    lane = lax.broadcasted_iota(jnp.int32, (tk, AUG_LANES), 1)
    firstq = _head_pair_masks((tk, 2 * HEAD))
    minus = []
    for h in range(2):
        lo = AUG_PIECES * (2 * pi + h)
        minus.append(jnp.where((lane >= lo) & (lane < lo + AUG_PIECES), -1.0, 0.0).astype(BF16))
    kpos = lax.broadcasted_iota(jnp.int32, (tk, tk), 0)
    qpos = lax.broadcasted_iota(jnp.int32, (tk, tk), 1)
    causal = kpos <= qpos

    def scores(j):
        slot = j % 2
        q = q_ref[j * tk:(j + 1) * tk, :]
        zero = jnp.zeros_like(q)
        for h in range(2):
            qh = jnp.where(firstq, q, zero) if h == 0 else jnp.where(firstq, zero, q)
            qa = jnp.concatenate([qh, minus[h]], axis=1)
            pm = None
            for kb in range(j + 1):
                s = _dot_nt(ka_scr[kb * tk:(kb + 1) * tk, :], qa)
                if kb == j:
                    s = jnp.where(causal, s, -jnp.inf)
                st_scr[slot, h, kb * tk:(kb + 1) * tk, :] = s
                part = jnp.max(s.reshape(tk // 8, 8, tk), axis=0)
                pm = part if pm is None else jnp.maximum(pm, part)
            pm_scr[slot, h] = pm

    def softmax_out(j):
        slot = j % 2
        outs = []
        for h in range(2):
            mx = jnp.max(pm_scr[slot, h], axis=0, keepdims=True)
            for kb in range(j + 1):
                rows = slice(kb * tk, (kb + 1) * tk)
                pt_scr[slot, h, rows, :] = jnp.exp2(st_scr[slot, h, rows, :] - mx).astype(BF16)
            acc = _dot(vt_scr[h, :, 0:(j + 1) * tk], pt_scr[slot, h, 0:(j + 1) * tk, :])
            outs.append(acc[0:HEAD] / acc[HEAD:HEAD + 1])
        o_ref[j * tk:(j + 1) * tk, :] = jnp.concatenate(outs, axis=0).T.astype(o_ref.dtype)

    scores(0)
    for j in range(nk):
        if j + 1 < nk:
            scores(j + 1)
        softmax_out(j)


def _attn_prompt(q, kt, vt, c_aug, seq):
    n, d = q.shape
    b = n // seq
    npair = d // (2 * HEAD)
    tk = min(seq, 256)
    assert seq % tk == 0
    blk = pl.BlockSpec((seq, 2 * HEAD), lambda bi, pi: (bi, pi))
    blk_t = pl.BlockSpec((None, 2 * HEAD, seq), lambda bi, pi: (bi, pi, 0))
    return pl.pallas_call(
        functools.partial(_attn_prompt_kernel, tk),
        grid=(b, npair),
        in_specs=[blk, blk_t, blk_t, pl.BlockSpec((None, seq, AUG_LANES), lambda bi, pi: (bi, 0, 0))],
        out_specs=blk,
        out_shape=jax.ShapeDtypeStruct((n, d), BF16),
        scratch_shapes=[pltpu.VMEM((seq, 2 * HEAD + AUG_LANES), BF16),
                        pltpu.VMEM((2, HEAD + ONES_ROWS, seq), BF16),
                        pltpu.VMEM((2, 2, seq, tk), F32),
                        pltpu.VMEM((2, 2, 8, tk), F32),
                        pltpu.VMEM((2, 2, seq, tk), BF16)],
        compiler_params=_cparams(("arbitrary", "arbitrary")),
        name="fox_attn_prompt",
    )(q, kt, vt, c_aug)


def _attn_sample_kernel(q_ref, kp_ref, vp_ref, kn_ref, vn_ref, ckp_ref, ckn_ref, o_ref):
    q = q_ref[...]
    first = _head_pair_masks(q.shape)
    zero = jnp.zeros_like(q)
    qh = (jnp.where(first, q, zero), jnp.where(first, zero, q))
    kp = kp_ref[...].astype(BF16)
    vp = vp_ref[...].astype(BF16)
    kn = kn_ref[...].astype(BF16)
    vn = vn_ref[...].astype(BF16)
    outs = []
    for h in range(2):
        sp = _dot(qh[h], kp) - ckp_ref[h:h + 1, :] * LOG2E
        sn = _dot(qh[h], kn) - ckn_ref[h:h + 1, :] * LOG2E
        qpos = lax.broadcasted_iota(jnp.int32, sn.shape, 0)
        kpos = lax.broadcasted_iota(jnp.int32, sn.shape, 1)
        sn = jnp.where(kpos <= qpos, sn, -jnp.inf)
        m = jnp.maximum(jnp.max(sp, axis=-1, keepdims=True), jnp.max(sn, axis=-1, keepdims=True))
        pp = jnp.exp2(sp - m)
        pn = jnp.exp2(sn - m)
        l = jnp.sum(pp, axis=-1, keepdims=True) + jnp.sum(pn, axis=-1, keepdims=True)
        outs.append((_dot_nt(pp.astype(BF16), vp) + _dot_nt(pn.astype(BF16), vn)) / l)
    o_ref[...] = jnp.where(first, outs[0], outs[1]).astype(o_ref.dtype)


def _attn_sample(q, kpt, vpt, knt, vnt, ckp_row, ckn_row, seq, past):
    n, d = q.shape
    b = n // seq
    npair = d // (2 * HEAD)
    ckp = ckp_row.reshape(b, npair, 2, past)
    ckn = ckn_row.reshape(b, npair, 2, seq)
    rows = pl.BlockSpec((seq, 2 * HEAD), lambda bi, pi: (bi, pi))
    new = pl.BlockSpec((None, 2 * HEAD, seq), lambda bi, pi: (bi, pi, 0))
    old = pl.BlockSpec((None, 2 * HEAD, past), lambda bi, pi: (bi, pi, 0))
    return pl.pallas_call(
        _attn_sample_kernel,
        grid=(b, npair),
        in_specs=[rows, old, old, new, new,
                  pl.BlockSpec((None, None, 2, past), lambda bi, pi: (bi, pi, 0, 0)),
                  pl.BlockSpec((None, None, 2, seq), lambda bi, pi: (bi, pi, 0, 0))],
        out_specs=rows,
        out_shape=jax.ShapeDtypeStruct((n, d), BF16),
        compiler_params=_cparams(("arbitrary", "arbitrary")),
        name="fox_attn_sample",
    )(q, kpt, vpt, knt, vnt, ckp, ckn)


def _oproj_kernel(o_ref, h_ref, w_ref, out_ref):
    out_ref[...] = h_ref[...] + _dot(o_ref[...], w_ref[...])


def _oproj(o, h, w):
    n, d = h.shape
    tm = _row_tile(n, n, 512)
    row = pl.BlockSpec((tm, d), lambda i: (i, 0))
    return pl.pallas_call(
        _oproj_kernel,
        grid=(n // tm,),
        in_specs=[row, row, _full(w.shape)],
        out_specs=row,
        out_shape=jax.ShapeDtypeStruct((n, d), F32),
        compiler_params=_cparams(("arbitrary",)),
        name="attn_oproj",
    )(o, h, w)


def _pack_state(s):
    b, h, v, k = s.shape
    return s.reshape(b, h // HEADS_PER_GROUP, HEADS_PER_GROUP, v, k).transpose(0, 1, 3, 2, 4).reshape(
        b, h // HEADS_PER_GROUP, v, HEADS_PER_GROUP * k)


def _unpack_state(sc):
    b, g, v, gk = sc.shape
    k = gk // HEADS_PER_GROUP
    return sc.reshape(b, g, v, HEADS_PER_GROUP, k).transpose(0, 1, 3, 2, 4).reshape(b, g * HEADS_PER_GROUP, v, k)


def _trunk(x, shift0, wkv0, past, w):
    b, t, d = x.shape
    n = b * t
    h = x.reshape(n, d)
    n_a = w["w_r"].shape[0]
    new_shift, new_wkv = [], []
    for l in range(n_a):
        r, lw, k, v, av, bv, g, xl = _rwkv_pre(
            h, shift0[l].reshape(b, 1, d), t, w["ln1_g"][l], w["mu"][l], w["w_r"][l], w["w_k"][l], w["w_v"][l],
            w["w1"][l], w["w2"][l], w["a1"][l], w["a2"][l], w["g1"][l], w["g2"][l],
            w["w0"][l], w["a0"][l], w["k_k"][l], w["k_a"][l])
        tp = -(-t // CHUNK) * CHUNK
        scan_in = (r, lw, k, v, av, bv)
        if tp != t:
            scan_in = tuple(jnp.pad(z.reshape(b, t, d), ((0, 0), (0, tp - t), (0, 0))).reshape(b * tp, d)
                            for z in scan_in)
        y, s_out = _wkv_scan(*scan_in, _pack_state(wkv0[l]), tp)
        if tp != t:
            y = y.reshape(b, tp, d)[:, :t].reshape(n, d)
        h = _rwkv_post(y, r, k, v, g, h, w["lnx_g"][l], w["lnx_b"][l], w["r_k"][l], w["w_o"][l])
        h = _ffn(h, w["ln2_g"][l], w["w_gate"][l], w["w_up"][l], w["w_down"][l])[0]
        new_shift.append(xl.reshape(b, d))
        new_wkv.append(_unpack_state(s_out))

    nh = d // HEAD
    n_b = w["w_q"].shape[0]
    scale = HEAD ** -0.5 * LOG2E
    kt_new, vt_new, logf, q = _kvq(h, t, w["kv_g"], w["ln1_g"][n_a], w["w_kvf_kt"], w["w_kvf_vt"], w["w_kvf_ft"],
                                   w["b_f"], w["w_q"][0], scale)
    zero_carry = jnp.zeros((b, nh, 1), F32)
    if past is None:
        _, c_aug, _ = _cumsum(logf, zero_carry)
    else:
        pk, pv, plf = past
        p = pk.shape[1]
        pkt = jnp.transpose(pk, (0, 2, 3, 1)).reshape(b, d, p).astype(F32)
        pvt = jnp.transpose(pv, (0, 2, 3, 1)).reshape(b, d, p).astype(F32)
        cp_row, _, cp_last = _cumsum(jnp.transpose(plf, (0, 2, 1)).astype(F32), zero_carry)
        c_row, _, _ = _cumsum(logf, cp_last)
    for j in range(n_b):
        l = n_a + j
        if j > 0:
            q = _q_only(h, w["ln1_g"][l], w["w_q"][j], scale)
        if past is None:
            o = _attn_prompt(q, kt_new, vt_new, c_aug, t)
        else:
            o = _attn_sample(q, pkt, pvt, kt_new, vt_new, cp_row, c_row, t, p)
        h = _oproj(o, h, w["w_ob"][j])
        last = j == n_b - 1
        res = _ffn(h, w["ln2_g"][l], w["w_gate"][l], w["w_up"][l], w["w_down"][l],
                   w["final_g"] if last else None)
        h = res[0]
        if last:
            y_out = res[1]
    k_new = jnp.transpose(kt_new.reshape(b, nh, HEAD, t), (0, 3, 1, 2))
    v_new = jnp.transpose(vt_new.reshape(b, nh, HEAD, t), (0, 3, 1, 2))
    return (y_out.reshape(b, t, d), jnp.stack(new_shift), jnp.stack(new_wkv),
            k_new, v_new, jnp.transpose(logf, (0, 2, 1)))


def _q_only_kernel(scale, h_ref, ln_ref, wq_ref, q_out):
    q_out[...] = (_dot(_rms(h_ref[...], ln_ref[...]).astype(BF16), wq_ref[...]) * scale).astype(BF16)


def _q_only(h, ln, wq, scale):
    n, d = h.shape
    tm = _row_tile(n, n, 512)
    row = pl.BlockSpec((tm, d), lambda i: (i, 0))
    return pl.pallas_call(
        functools.partial(_q_only_kernel, scale),
        grid=(n // tm,),
        in_specs=[row, _full((1, d)), _full(wq.shape)],
        out_specs=row,
        out_shape=jax.ShapeDtypeStruct((n, d), BF16),
        compiler_params=_cparams(("arbitrary",)),
        name="q_proj",
    )(h, ln, wq)


def kernel(x_prompt, x_sample, state_shift, state_wkv, cache_k, cache_v, cache_logf, ln1_g, ln2_g, w_gate, w_up, w_down, mu, w_r, w_k, w_v, w_o, w0, w1, w2, a0, a1, a2, g1, g2, k_k, k_a, r_k, lnx_g, lnx_b, kv_g, w_kvf, b_f, w_q, w_ob, final_g):
    d = x_prompt.shape[-1]
    nh = d // HEAD
    n_a = w_r.shape[0]
    bf = lambda z: z.astype(BF16)
    vec = lambda z: z.reshape(z.shape[:-1] + (1, d)).astype(F32)
    w = dict(
        ln1_g=vec(ln1_g), ln2_g=vec(ln2_g), w_gate=bf(w_gate), w_up=bf(w_up), w_down=bf(w_down),
        mu=mu.astype(F32), w_r=bf(w_r), w_k=bf(w_k), w_v=bf(w_v), w_o=bf(w_o),
        w0=vec(w0), w1=bf(w1), w2=bf(w2), a0=vec(a0), a1=bf(a1), a2=bf(a2), g1=bf(g1), g2=bf(g2),
        k_k=vec(k_k), k_a=vec(k_a),
        r_k=r_k.reshape(n_a, 1, d).astype(F32),
        lnx_g=vec(lnx_g), lnx_b=vec(lnx_b), kv_g=vec(kv_g),
        w_kvf_kt=bf(w_kvf[:, :d].T), w_kvf_vt=bf(w_kvf[:, d:2 * d].T), w_kvf_ft=bf(w_kvf[:, 2 * d:].T),
        b_f=b_f.reshape(nh, 1).astype(F32), w_q=bf(w_q), w_ob=bf(w_ob), final_g=vec(final_g),
    )
    bp = x_prompt.shape[0]
    dt = x_prompt.dtype
    y_p, shift_p, wkv_p, k_p, v_p, lf_p = _trunk(
        x_prompt, jnp.zeros((n_a, bp, d), dt), jnp.zeros((n_a, bp, nh, HEAD, HEAD), dt), None, w)
    y_s, shift_s, wkv_s, k_s, v_s, lf_s = _trunk(
        x_sample, state_shift, state_wkv, (cache_k, cache_v, cache_logf), w)
    return (y_p, y_s, shift_p, wkv_p, k_p, v_p, lf_p, shift_s, wkv_s, k_s, v_s, lf_s)
```

```python
import functools

import jax
import jax.numpy as jnp
from jax import lax
from jax.experimental import pallas as pl
from jax.experimental.pallas import tpu as pltpu

F32 = jnp.float32
BF16 = jnp.bfloat16

HEAD = 64
LANE_GROUP = 256
HEADS_PER_GROUP = LANE_GROUP // HEAD
CHUNK = 64
RMS_EPS = 1e-5
GN_EPS = 64e-5
LOG2E = 1.4426950408889634
NEG_EXP_M_HALF = -0.6065306597126334
AUG_LANES = 128
AUG_PIECES = 3
ONES_ROWS = 16
VMEM_LIMIT = 56 * 1024 * 1024


def _cparams(sem):
    return pltpu.CompilerParams(dimension_semantics=sem, vmem_limit_bytes=VMEM_LIMIT)


def _dot(a, b):
    return jnp.dot(a, b, preferred_element_type=F32)


def _dot_nt(a, b):
    return lax.dot_general(a, b, (((1,), (1,)), ((), ())), preferred_element_type=F32)


def _dot_tn(a, b):
    return lax.dot_general(a, b, (((0,), (0,)), ((), ())), preferred_element_type=F32)


def _split(x, parts):
    out = []
    rem = x
    for _ in range(parts):
        p = rem.astype(BF16)
        out.append(p)
        rem = rem - p.astype(F32)
    return out


def _mm(dot, a, b, pa=1, pb=1):
    a_parts = _split(a, pa)
    b_parts = _split(b, pb)
    acc = None
    for i, ap in enumerate(a_parts):
        for j, bp in enumerate(b_parts):
            if i + j >= max(pa, pb):
                continue
            t = dot(ap, bp)
            acc = t if acc is None else acc + t
    return acc


def _rms(x, g):
    ms = jnp.mean(x * x, axis=-1, keepdims=True)
    return x * lax.rsqrt(ms + RMS_EPS) * g


def _sigmoid(z):
    return 1.0 / (1.0 + jnp.exp2(z * (-LOG2E)))


def _softplus(z):
    return jnp.maximum(z, 0.0) + jnp.log(1.0 + jnp.exp(-jnp.abs(z)))


def _group_ones():
    r = lax.broadcasted_iota(jnp.int32, (LANE_GROUP, LANE_GROUP), 0) // HEAD
    c = lax.broadcasted_iota(jnp.int32, (LANE_GROUP, LANE_GROUP), 1) // HEAD
    return jnp.where(r == c, 1.0, 0.0).astype(BF16)


def _head_sum(x, ones_bd):
    outs = []
    for g in range(x.shape[1] // LANE_GROUP):
        outs.append(_dot(x[:, g * LANE_GROUP:(g + 1) * LANE_GROUP].astype(BF16), ones_bd))
    return jnp.concatenate(outs, axis=-1)


def _row_tile(n_rows, seq, cap):
    t = min(cap, seq)
    while seq % t:
        t //= 2
    assert t % 8 == 0 and n_rows % t == 0
    return t


def _full(shape):
    return pl.BlockSpec(shape, lambda *_: (0,) * len(shape))


def _rwkv_pre_kernel(tiles_per_seq, x_ref, xp_ref, sh_ref, ln_ref, mu_ref, wr_ref, wk_ref, wv_ref,
                     w1_ref, w2_ref, a1_ref, a2_ref, g1_ref, g2_ref, w0_ref, a0_ref, kk_ref, ka_ref,
                     r_out, lw_out, k_out, v_out, av_out, bv_out, g_out, xl_out):
    i = pl.program_id(0)
    ln = ln_ref[...]
    xn = _rms(x_ref[...], ln)
    tm = xn.shape[0]
    prev_tile_last = _rms(xp_ref[...], ln)[7:8, :]
    prev = jnp.where(i % tiles_per_seq == 0, sh_ref[...], prev_tile_last)
    row = lax.broadcasted_iota(jnp.int32, xn.shape, 0)
    x_prev = jnp.where(row == 0, prev, pltpu.roll(xn, 1, 0))
    xx = x_prev - xn
    xl_out[...] = xn[tm - 1:tm, :]

    sub = min(tm, 128)
    subs = [slice(s * sub, (s + 1) * sub) for s in range(tm // sub)]
    ones_bd = _group_ones()
    mixes = [[(xn[rows] + xx[rows] * mu_ref[j:j + 1, :]).astype(BF16) for j in range(6)] for rows in subs]
    r, k, v, lin = [], [], [], []
    for m in mixes:
        r.append(_dot(m[0], wr_ref[...]))
        k.append(_dot(m[2], wk_ref[...]))
        v.append(_dot(m[3], wv_ref[...]))
        lin.append((_dot(m[1], w1_ref[...]), _dot(m[4], a1_ref[...]), _dot(m[5], g1_ref[...])))
    act = [(jnp.tanh(lw).astype(BF16), la.astype(BF16), _sigmoid(lg).astype(BF16)) for lw, la, lg in lin]
    lout = [(_dot(aw, w2_ref[...]), _dot(aa, a2_ref[...]), _dot(ag, g2_ref[...])) for aw, aa, ag in act]
    kk = [ks * kk_ref[...] for ks in k]
    nsq = [_head_sum(kks * kks, ones_bd) for kks in kk]
    for s, rows in enumerate(subs):
        wl, al, g = lout[s]
        asig = _sigmoid(a0_ref[...] + al)
        kkn = kk[s] / jnp.maximum(jnp.sqrt(nsq[s]), 1e-12)
        r_out[rows, :] = r[s].astype(r_out.dtype)
        lw_out[rows, :] = NEG_EXP_M_HALF * _sigmoid(w0_ref[...] + wl)
        k_out[rows, :] = (k[s] * (1.0 + (asig - 1.0) * ka_ref[...])).astype(k_out.dtype)
        v_out[rows, :] = v[s].astype(v_out.dtype)
        av_out[rows, :] = (-kkn).astype(av_out.dtype)
        bv_out[rows, :] = (kkn * asig).astype(bv_out.dtype)
        g_out[rows, :] = g.astype(g_out.dtype)


def _rwkv_pre(x, shift0, seq, ln, mu, wr, wk, wv, w1, w2, a1, a2, g1, g2, w0, a0, k_k, k_a):
    n, d = x.shape
    b = n // seq
    tm = _row_tile(n, seq, 512)
    tps = seq // tm
    row_spec = pl.BlockSpec((tm, d), lambda i: (i, 0))
    vec = _full((1, d))
    outs = pl.pallas_call(
        functools.partial(_rwkv_pre_kernel, tps),
        grid=(n // tm,),
        in_specs=[
            row_spec,
            pl.BlockSpec((8, d), lambda i: (jnp.maximum(i * (tm // 8) - 1, 0), 0)),
            pl.BlockSpec((None, 1, d), lambda i: (i // tps, 0, 0)),
            vec, _full(mu.shape),
            _full(wr.shape), _full(wk.shape), _full(wv.shape),
            _full(w1.shape), _full(w2.shape), _full(a1.shape), _full(a2.shape),
            _full(g1.shape), _full(g2.shape),
            vec, vec, vec, vec,
        ],
        out_specs=[row_spec] * 7 + [pl.BlockSpec((None, 1, d), lambda i: (i // tps, 0, 0))],
        out_shape=[jax.ShapeDtypeStruct((n, d), F32 if j == 1 else BF16) for j in range(7)]
        + [jax.ShapeDtypeStruct((b, 1, d), F32)],
        compiler_params=_cparams(("arbitrary",)),
        name="rwkv_pre",
    )(x, x, shift0, ln, mu, wr, wk, wv, w1, w2, a1, a2, g1, g2, w0, a0, k_k, k_a)
    return outs


def _scan_kernel(r_ref, lw_ref, k_ref, v_ref, a_ref, b_ref, s0_ref, y_ref, so_ref, s_scr):
    c = pl.program_id(1)
    C = CHUNK
    G = LANE_GROUP
    ri = lax.broadcasted_iota(jnp.int32, (G, G), 0)
    ci = lax.broadcasted_iota(jnp.int32, (G, G), 1)
    bd = (ri // HEAD) == (ci // HEAD)

    def expand(x):
        return jnp.where(bd, jnp.concatenate([x] * HEADS_PER_GROUP, axis=0), 0.0)

    n_seq, n_grp = s0_ref.shape[0], s0_ref.shape[1]
    chains = [(s, g) for s in range(n_seq) for g in range(n_grp)]
    n_groups = len(chains)

    @pl.when(c == 0)
    def _():
        for i, (s, g) in enumerate(chains):
            s_scr[i] = expand(s0_ref[s, g])

    tpos = lax.broadcasted_iota(jnp.int32, (C, G), 0)
    spos = lax.broadcasted_iota(jnp.int32, (C, G), 1) % C
    strict = spos < tpos
    incl = spos <= tpos
    tri = jnp.where(lax.broadcasted_iota(jnp.int32, (C, C), 0) >= lax.broadcasted_iota(jnp.int32, (C, C), 1),
                    1.0, 0.0).astype(BF16)
    steps = C.bit_length() - 1

    groups = range(n_groups)
    sls = [(s, slice(None), slice(g * G, (g + 1) * G)) for s, g in chains]
    lw = [lw_ref[sl] for sl in sls]
    v = [v_ref[sl] for sl in sls]
    cs = [_mm(_dot, tri, lw[g], 1, 2) for g in groups]
    p_in = [jnp.exp(cs[g]) for g in groups]
    p_inv = [jnp.exp(-cs[g]) for g in groups]
    p_end = [p_in[g][C - 1:C, :] for g in groups]
    rt = [r_ref[sls[g]] * p_in[g] for g in groups]
    at = [a_ref[sls[g]] * jnp.exp(cs[g] - lw[g]) for g in groups]
    bt = [b_ref[sls[g]] * p_inv[g] for g in groups]
    kt = [k_ref[sls[g]] * p_inv[g] for g in groups]
    S = [s_scr[g] for g in groups]
    la = [jnp.concatenate([at[g], rt[g]], axis=0) for g in groups]
    ab = [_mm(_dot_nt, la[g], expand(bt[g])) for g in groups]
    ak = [_mm(_dot_nt, la[g], expand(kt[g])) for g in groups]
    npow = [jnp.where(strict, ab[g][:C], 0.0) for g in groups]
    a_rb = [jnp.where(incl, ab[g][C:], 0.0) for g in groups]
    a_k = [jnp.where(jnp.concatenate([strict, incl], axis=0), ak[g], 0.0) for g in groups]
    from_v = [_mm(_dot, a_k[g], expand(v[g])) for g in groups]
    from_s = [_mm(_dot_nt, la[g], S[g]) for g in groups]
    w0 = [from_s[g][:C] + from_v[g][:C] for g in groups]
    y0 = [from_s[g][C:] + from_v[g][C:] for g in groups]
    eye = jnp.where(spos == tpos, 1.0, 0.0)
    tinv = [eye + npow[g] for g in groups]
    npow = [_mm(_dot, npow[g], expand(npow[g])) for g in groups]
    for i in range(1, steps):
        last = i + 1 == steps
        lhs = [tinv[g] if last else jnp.concatenate([tinv[g], npow[g]], axis=0) for g in groups]
        prod = [_mm(_dot, lhs[g], expand(npow[g])) for g in groups]
        tinv = [tinv[g] + prod[g][:C] for g in groups]
        if not last:
            npow = [prod[g][C:] for g in groups]
    u = [_mm(_dot, tinv[g], expand(w0[g])) for g in groups]
    for g in groups:
        y_ref[sls[g]] = y0[g] + _mm(_dot, a_rb[g], expand(u[g]))
    for g in groups:
        upd = _mm(_dot_tn, jnp.concatenate([u[g], v[g]], axis=0),
                  jnp.concatenate([bt[g], kt[g]], axis=0) * p_end[g])
        s_scr[g] = S[g] * p_end[g] + jnp.where(bd, upd, 0.0)

    @pl.when(c == pl.num_programs(1) - 1)
    def _():
        for i, (s, g) in enumerate(chains):
            acc = s_scr[i, 0:HEAD, :]
            for h in range(1, HEADS_PER_GROUP):
                acc = acc + s_scr[i, h * HEAD:(h + 1) * HEAD, :]
            so_ref[s, g] = acc


def _wkv_scan(r, lw, k, v, av, bv, s0c, seq):
    n, d = r.shape
    b = n // seq
    ng = d // LANE_GROUP
    nc = seq // CHUNK
    sps = 2 if b % 2 == 0 else 1
    blk = pl.BlockSpec((sps, CHUNK, d), lambda bi, ci: (bi, ci, 0))
    st = pl.BlockSpec((sps, ng, HEAD, LANE_GROUP), lambda bi, ci: (bi, 0, 0, 0))
    y, s_out = pl.pallas_call(
        _scan_kernel,
        grid=(b // sps, nc),
        in_specs=[blk] * 6 + [st],
        out_specs=[blk, st],
        out_shape=[jax.ShapeDtypeStruct((b, seq, d), F32), jax.ShapeDtypeStruct(s0c.shape, F32)],
        scratch_shapes=[pltpu.VMEM((sps * ng, LANE_GROUP, LANE_GROUP), F32)],
        compiler_params=_cparams(("arbitrary", "arbitrary")),
        name="wkv_scan",
    )(*(z.reshape(b, seq, d) for z in (r, lw, k, v, av, bv)), s0c)
    return y.reshape(n, d), s_out


def _rwkv_post_kernel(y_ref, r_ref, k_ref, v_ref, g_ref, h_ref, lg_ref, lb_ref, rk_ref, wo_ref, o_ref):
    ones_bd = _group_ones()
    tm = y_ref.shape[0]
    sub = min(tm, 128)
    subs = [slice(s * sub, (s + 1) * sub) for s in range(tm // sub)]
    y = [y_ref[rows, :] for rows in subs]
    mean = [_head_sum(ys, ones_bd) * (1.0 / HEAD) for ys in y]
    rk = [_head_sum(r_ref[rows, :].astype(F32) * k_ref[rows, :].astype(F32) * rk_ref[...], ones_bd) for rows in subs]
    yc = [y[s] - mean[s] for s in range(len(subs))]
    var = [_head_sum(c * c, ones_bd) * (1.0 / HEAD) for c in yc]
    for s, rows in enumerate(subs):
        yn = yc[s] * lax.rsqrt(var[s] + GN_EPS) * lg_ref[...] + lb_ref[...]
        yn = yn + rk[s] * v_ref[rows, :].astype(F32)
        gated = (yn * g_ref[rows, :].astype(F32)).astype(BF16)
        o_ref[rows, :] = h_ref[rows, :] + _dot(gated, wo_ref[...])


def _rwkv_post(y, r, k, v, g, h, lnx_g, lnx_b, r_k, wo):
    n, d = y.shape
    tm = _row_tile(n, n, 512)
    row = pl.BlockSpec((tm, d), lambda i: (i, 0))
    vec = _full((1, d))
    return pl.pallas_call(
        _rwkv_post_kernel,
        grid=(n // tm,),
        in_specs=[row] * 6 + [vec, vec, vec, _full(wo.shape)],
        out_specs=row,
        out_shape=jax.ShapeDtypeStruct((n, d), F32),
        compiler_params=_cparams(("arbitrary",)),
        name="rwkv_post",
    )(y, r, k, v, g, h, lnx_g, lnx_b, r_k, wo)


def _ffn_kernel(final, proj, tf, h_ref, ln_ref, wg_ref, wu_ref, wd_ref, *rest):
    rest = list(rest)
    a_ref, wa_ref = (rest.pop(0), rest.pop(0)) if proj else (None, None)
    fg_ref = rest.pop(0) if final else None
    o_ref = rest.pop(0)
    y_ref = rest.pop(0) if final else None
    act_scr, = rest
    h = h_ref[...]
    if proj:
        h = h + _dot(a_ref[...], wa_ref[...])
    xn = _rms(h, ln_ref[...]).astype(BF16)
    for c in range(act_scr.shape[1] // tf):
        cols = slice(c * tf, (c + 1) * tf)
        gate = _dot(xn, wg_ref[:, cols])
        up = _dot(xn, wu_ref[:, cols])
        act_scr[:, cols] = (gate * _sigmoid(gate) * up).astype(BF16)
    out = h + _dot(act_scr[...], wd_ref[...])
    o_ref[...] = out
    if final:
        y_ref[...] = _rms(out, fg_ref[...])


def _ffn(h, ln, wg, wu, wd, final_g=None, proj=None):
    n, d = h.shape
    dff = wg.shape[1]
    tm = _row_tile(n, n, 512 if proj is not None else 1024)
    tf = 256
    assert dff % tf == 0
    final = final_g is not None
    row = pl.BlockSpec((tm, d), lambda i: (i, 0))
    vec = pl.BlockSpec((1, d), lambda i: (0, 0))
    once = pl.Buffered(1)
    in_specs = [row, vec,
                pl.BlockSpec((d, dff), lambda i: (0, 0), pipeline_mode=once),
                pl.BlockSpec((d, dff), lambda i: (0, 0), pipeline_mode=once),
                pl.BlockSpec((dff, d), lambda i: (0, 0), pipeline_mode=once)]
    args = [h, ln, wg, wu, wd]
    out_specs = [row]
    out_shape = [jax.ShapeDtypeStruct((n, d), F32)]
    if proj is not None:
        in_specs += [row, pl.BlockSpec((d, d), lambda i: (0, 0), pipeline_mode=once)]
        args += list(proj)
    if final:
        in_specs.append(vec)
        args.append(final_g)
        out_specs.append(row)
        out_shape.append(jax.ShapeDtypeStruct((n, d), F32))
    res = pl.pallas_call(
        functools.partial(_ffn_kernel, final, proj is not None, tf),
        grid=(n // tm,),
        in_specs=in_specs,
        out_specs=out_specs,
        out_shape=out_shape,
        scratch_shapes=[pltpu.VMEM((tm, dff), BF16)],
        compiler_params=_cparams(("arbitrary",)),
        name="ffn_final" if final else "ffn",
    )(*args)
    return res


def _kvq_kernel(scale, h_ref, kvg_ref, ln_ref, wkt_ref, wvt_ref, wft_ref, bf_ref, wq_ref,
                kt_out, vt_out, f_out, q_out):
    h = h_ref[...]
    ms = jnp.mean(h * h, axis=-1, keepdims=True)
    hn = h * lax.rsqrt(ms + RMS_EPS)
    xkv = (hn * kvg_ref[...]).astype(BF16)
    xq = (hn * ln_ref[...]).astype(BF16)
    kt_out[...] = _dot_nt(wkt_ref[...], xkv)
    vt_out[...] = _dot_nt(wvt_ref[...], xkv)
    z = _dot_nt(wft_ref[...], xkv) + bf_ref[...]
    f_out[...] = -_softplus(-z)
    q_out[...] = (_dot(xq, wq_ref[...]) * scale).astype(BF16)


def _kvq(h, seq, kv_g, ln, wkt, wvt, wft, b_f, wq, scale):
    n, d = h.shape
    b = n // seq
    nh = wft.shape[0]
    tm = _row_tile(n, seq, 512)
    tps = seq // tm
    row = pl.BlockSpec((tm, d), lambda i: (i, 0))
    vec = _full((1, d))
    col = pl.BlockSpec((None, d, tm), lambda i: (i // tps, 0, i % tps))
    return pl.pallas_call(
        functools.partial(_kvq_kernel, scale),
        grid=(n // tm,),
        in_specs=[row, vec, vec, _full(wkt.shape), _full(wvt.shape), _full(wft.shape), _full((nh, 1)), _full(wq.shape)],
        out_specs=[col, col, pl.BlockSpec((None, nh, tm), lambda i: (i // tps, 0, i % tps)), row],
        out_shape=[jax.ShapeDtypeStruct((b, d, seq), F32), jax.ShapeDtypeStruct((b, d, seq), F32),
                   jax.ShapeDtypeStruct((b, nh, seq), F32), jax.ShapeDtypeStruct((n, d), BF16)],
        compiler_params=_cparams(("arbitrary",)),
        name="kvq_proj",
    )(h, kv_g, ln, wkt, wvt, wft, b_f, wq)


def _cumsum_kernel(blk, f_ref, c0_ref, row_ref, aug_ref, last_ref):
    nh, t = f_ref.shape
    triu = jnp.where(lax.broadcasted_iota(jnp.int32, (blk, blk), 0) <= lax.broadcasted_iota(jnp.int32, (blk, blk), 1),
                     1.0, 0.0).astype(BF16)
    head = lax.broadcasted_iota(jnp.int32, (nh, AUG_LANES), 0)
    lane = lax.broadcasted_iota(jnp.int32, (nh, AUG_LANES), 1)
    place = [jnp.where(lane == AUG_PIECES * head + i, 1.0, 0.0).astype(BF16) for i in range(AUG_PIECES)]
    carry = c0_ref[...]
    for j in range(t // blk):
        cols = slice(j * blk, (j + 1) * blk)
        cb = _mm(_dot, f_ref[:, cols], triu, 3, 1) + carry
        row_ref[:, cols] = cb
        pieces = _split(cb * LOG2E, AUG_PIECES)
        aug = _dot_tn(pieces[0], place[0])
        for i in range(1, AUG_PIECES):
            aug = aug + _dot_tn(pieces[i], place[i])
        aug_ref[cols, :] = aug.astype(BF16)
        carry = cb[:, blk - 1:blk]
    last_ref[...] = carry


def _cumsum(logf, c0):
    b, nh, t = logf.shape
    blk = min(t, 256)
    assert t % blk == 0 and AUG_PIECES * nh <= AUG_LANES
    row = pl.BlockSpec((None, nh, t), lambda i: (i, 0, 0))
    one = pl.BlockSpec((None, nh, 1), lambda i: (i, 0, 0))
    return pl.pallas_call(
        functools.partial(_cumsum_kernel, blk),
        grid=(b,),
        in_specs=[row, one],
        out_specs=[row, pl.BlockSpec((None, t, AUG_LANES), lambda i: (i, 0, 0)), one],
        out_shape=[jax.ShapeDtypeStruct((b, nh, t), F32), jax.ShapeDtypeStruct((b, t, AUG_LANES), BF16),
                   jax.ShapeDtypeStruct((b, nh, 1), F32)],
        compiler_params=_cparams(("arbitrary",)),
        name="logf_cumsum",
    )(logf, c0)


def _head_pair_masks(shape):
    lane = lax.broadcasted_iota(jnp.int32, shape, len(shape) - 1)
    return lane < HEAD


def _attn_prompt_kernel(tk, q_ref, k_ref, v_ref, aug_ref, o_ref, ka_scr, vt_scr, st_scr, pm_scr, pt_scr):
    pi = pl.program_id(1)
    seq = q_ref.shape[0]
    nk = seq // tk
    vrows = HEAD + ONES_ROWS

    for kb in range(nk):
        ka_scr[kb * tk:(kb + 1) * tk, 0:2 * HEAD] = k_ref[:, kb * tk:(kb + 1) * tk].T.astype(BF16)
    ka_scr[:, 2 * HEAD:] = aug_ref[...]
    for h in range(2):
        vt_scr[h, 0:HEAD, :] = v_ref[h * HEAD:(h + 1) * HEAD, :].astype(BF16)
        vt_scr[h, HEAD:vrows, :] = jnp.ones((ONES_ROWS, seq), BF16)

    lane = lax.broadcasted_iota(jnp.int32, (tk, AUG_LANES), 1)
    firstq = _head_pair_masks((tk, 2 * HEAD))
    minus = []
    for h in range(2):
        lo = AUG_PIECES * (2 * pi + h)
        minus.append(jnp.where((lane >= lo) & (lane < lo + AUG_PIECES), -1.0, 0.0).astype(BF16))
    kpos = lax.broadcasted_iota(jnp.int32, (tk, tk), 0)
    qpos = lax.broadcasted_iota(jnp.int32, (tk, tk), 1)
    causal = kpos <= qpos

    def scores(j):
        slot = j % 2
        q = q_ref[j * tk:(j + 1) * tk, :]
        zero = jnp.zeros_like(q)
        for h in range(2):
            qh = jnp.where(firstq, q, zero) if h == 0 else jnp.where(firstq, zero, q)
            qa = jnp.concatenate([qh, minus[h]], axis=1)
            pm = None
            for kb in range(j + 1):
                s = _dot_nt(ka_scr[kb * tk:(kb + 1) * tk, :], qa)
                if kb == j:
                    s = jnp.where(causal, s, -jnp.inf)
                st_scr[slot, h, kb * tk:(kb + 1) * tk, :] = s
                part = jnp.max(s.reshape(tk // 8, 8, tk), axis=0)
                pm = part if pm is None else jnp.maximum(pm, part)
            pm_scr[slot, h] = pm

    def softmax_out(j):
        slot = j % 2
        outs = []
        for h in range(2):
            mx = jnp.max(pm_scr[slot, h], axis=0, keepdims=True)
            for kb in range(j + 1):
                rows = slice(kb * tk, (kb + 1) * tk)
                pt_scr[slot, h, rows, :] = jnp.exp2(st_scr[slot, h, rows, :] - mx).astype(BF16)
            acc = _dot(vt_scr[h, :, 0:(j + 1) * tk], pt_scr[slot, h, 0:(j + 1) * tk, :])
            outs.append(acc[0:HEAD] / acc[HEAD:HEAD + 1])
        o_ref[j * tk:(j + 1) * tk, :] = jnp.concatenate(outs, axis=0).T.astype(o_ref.dtype)

    scores(0)
    for j in range(nk):
        if j + 1 < nk:
            scores(j + 1)
        softmax_out(j)


def _attn_prompt(q, kt, vt, c_aug, seq):
    n, d = q.shape
    b = n // seq
    npair = d // (2 * HEAD)
    tk = min(seq, 256)
    assert seq % tk == 0
    blk = pl.BlockSpec((seq, 2 * HEAD), lambda bi, pi: (bi, pi))
    blk_t = pl.BlockSpec((None, 2 * HEAD, seq), lambda bi, pi: (bi, pi, 0))
    return pl.pallas_call(
        functools.partial(_attn_prompt_kernel, tk),
        grid=(b, npair),
        in_specs=[blk, blk_t, blk_t, pl.BlockSpec((None, seq, AUG_LANES), lambda bi, pi: (bi, 0, 0))],
        out_specs=blk,
        out_shape=jax.ShapeDtypeStruct((n, d), BF16),
        scratch_shapes=[pltpu.VMEM((seq, 2 * HEAD + AUG_LANES), BF16),
                        pltpu.VMEM((2, HEAD + ONES_ROWS, seq), BF16),
                        pltpu.VMEM((2, 2, seq, tk), F32),
                        pltpu.VMEM((2, 2, 8, tk), F32),
                        pltpu.VMEM((2, 2, seq, tk), BF16)],
        compiler_params=_cparams(("arbitrary", "arbitrary")),
        name="fox_attn_prompt",
    )(q, kt, vt, c_aug)


def _attn_sample_kernel(q_ref, kp_ref, vp_ref, kn_ref, vn_ref, ckp_ref, ckn_ref, o_ref):
    q = q_ref[...]
    first = _head_pair_masks(q.shape)
    zero = jnp.zeros_like(q)
    qh = (jnp.where(first, q, zero), jnp.where(first, zero, q))
    kp = kp_ref[...].astype(BF16)
    vp = vp_ref[...].astype(BF16)
    kn = kn_ref[...].astype(BF16)
    vn = vn_ref[...].astype(BF16)
    outs = []
    for h in range(2):
        sp = _dot(qh[h], kp) - ckp_ref[h:h + 1, :] * LOG2E
        sn = _dot(qh[h], kn) - ckn_ref[h:h + 1, :] * LOG2E
        qpos = lax.broadcasted_iota(jnp.int32, sn.shape, 0)
        kpos = lax.broadcasted_iota(jnp.int32, sn.shape, 1)
        sn = jnp.where(kpos <= qpos, sn, -jnp.inf)
        m = jnp.maximum(jnp.max(sp, axis=-1, keepdims=True), jnp.max(sn, axis=-1, keepdims=True))
        pp = jnp.exp2(sp - m)
        pn = jnp.exp2(sn - m)
        l = jnp.sum(pp, axis=-1, keepdims=True) + jnp.sum(pn, axis=-1, keepdims=True)
        outs.append((_dot_nt(pp.astype(BF16), vp) + _dot_nt(pn.astype(BF16), vn)) / l)
    o_ref[...] = jnp.where(first, outs[0], outs[1]).astype(o_ref.dtype)


def _attn_sample(q, kpt, vpt, knt, vnt, ckp_row, ckn_row, seq, past):
    n, d = q.shape
    b = n // seq
    npair = d // (2 * HEAD)
    ckp = ckp_row.reshape(b, npair, 2, past)
    ckn = ckn_row.reshape(b, npair, 2, seq)
    rows = pl.BlockSpec((seq, 2 * HEAD), lambda bi, pi: (bi, pi))
    new = pl.BlockSpec((None, 2 * HEAD, seq), lambda bi, pi: (bi, pi, 0))
    old = pl.BlockSpec((None, 2 * HEAD, past), lambda bi, pi: (bi, pi, 0))
    return pl.pallas_call(
        _attn_sample_kernel,
        grid=(b, npair),
        in_specs=[rows, old, old, new, new,
                  pl.BlockSpec((None, None, 2, past), lambda bi, pi: (bi, pi, 0, 0)),
                  pl.BlockSpec((None, None, 2, seq), lambda bi, pi: (bi, pi, 0, 0))],
        out_specs=rows,
        out_shape=jax.ShapeDtypeStruct((n, d), BF16),
        compiler_params=_cparams(("arbitrary", "arbitrary")),
        name="fox_attn_sample",
    )(q, kpt, vpt, knt, vnt, ckp, ckn)


def _pack_state(s):
    b, h, v, k = s.shape
    return s.reshape(b, h // HEADS_PER_GROUP, HEADS_PER_GROUP, v, k).transpose(0, 1, 3, 2, 4).reshape(
        b, h // HEADS_PER_GROUP, v, HEADS_PER_GROUP * k)


def _unpack_state(sc):
    b, g, v, gk = sc.shape
    k = gk // HEADS_PER_GROUP
    return sc.reshape(b, g, v, HEADS_PER_GROUP, k).transpose(0, 1, 3, 2, 4).reshape(b, g * HEADS_PER_GROUP, v, k)


def _trunk(x, shift0, wkv0, past, w):
    b, t, d = x.shape
    n = b * t
    h = x.reshape(n, d)
    n_a = w["w_r"].shape[0]
    new_shift, new_wkv = [], []
    for l in range(n_a):
        r, lw, k, v, av, bv, g, xl = _rwkv_pre(
            h, shift0[l].reshape(b, 1, d), t, w["ln1_g"][l], w["mu"][l], w["w_r"][l], w["w_k"][l], w["w_v"][l],
            w["w1"][l], w["w2"][l], w["a1"][l], w["a2"][l], w["g1"][l], w["g2"][l],
            w["w0"][l], w["a0"][l], w["k_k"][l], w["k_a"][l])
        tp = -(-t // CHUNK) * CHUNK
        scan_in = (r, lw, k, v, av, bv)
        if tp != t:
            scan_in = tuple(jnp.pad(z.reshape(b, t, d), ((0, 0), (0, tp - t), (0, 0))).reshape(b * tp, d)
                            for z in scan_in)
        y, s_out = _wkv_scan(*scan_in, _pack_state(wkv0[l]), tp)
        if tp != t:
            y = y.reshape(b, tp, d)[:, :t].reshape(n, d)
        h = _rwkv_post(y, r, k, v, g, h, w["lnx_g"][l], w["lnx_b"][l], w["r_k"][l], w["w_o"][l])
        h = _ffn(h, w["ln2_g"][l], w["w_gate"][l], w["w_up"][l], w["w_down"][l])[0]
        new_shift.append(xl.reshape(b, d))
        new_wkv.append(_unpack_state(s_out))

    nh = d // HEAD
    n_b = w["w_q"].shape[0]
    scale = HEAD ** -0.5 * LOG2E
    kt_new, vt_new, logf, q = _kvq(h, t, w["kv_g"], w["ln1_g"][n_a], w["w_kvf_kt"], w["w_kvf_vt"], w["w_kvf_ft"],
                                   w["b_f"], w["w_q"][0], scale)
    zero_carry = jnp.zeros((b, nh, 1), F32)
    if past is None:
        _, c_aug, _ = _cumsum(logf, zero_carry)
    else:
        pk, pv, plf = past
        p = pk.shape[1]
        pkt = jnp.transpose(pk, (0, 2, 3, 1)).reshape(b, d, p).astype(F32)
        pvt = jnp.transpose(pv, (0, 2, 3, 1)).reshape(b, d, p).astype(F32)
        cp_row, _, cp_last = _cumsum(jnp.transpose(plf, (0, 2, 1)).astype(F32), zero_carry)
        c_row, _, _ = _cumsum(logf, cp_last)
    for j in range(n_b):
        l = n_a + j
        if j > 0:
            q = _q_only(h, w["ln1_g"][l], w["w_q"][j], scale)
        if past is None:
            o = _attn_prompt(q, kt_new, vt_new, c_aug, t)
        else:
            o = _attn_sample(q, pkt, pvt, kt_new, vt_new, cp_row, c_row, t, p)
        last = j == n_b - 1
        res = _ffn(h, w["ln2_g"][l], w["w_gate"][l], w["w_up"][l], w["w_down"][l],
                   w["final_g"] if last else None, proj=(o, w["w_ob"][j]))
        h = res[0]
        if last:
            y_out = res[1]
    k_new = jnp.transpose(kt_new.reshape(b, nh, HEAD, t), (0, 3, 1, 2))
    v_new = jnp.transpose(vt_new.reshape(b, nh, HEAD, t), (0, 3, 1, 2))
    return (y_out.reshape(b, t, d), jnp.stack(new_shift), jnp.stack(new_wkv),
            k_new, v_new, jnp.transpose(logf, (0, 2, 1)))


def _q_only_kernel(scale, h_ref, ln_ref, wq_ref, q_out):
    q_out[...] = (_dot(_rms(h_ref[...], ln_ref[...]).astype(BF16), wq_ref[...]) * scale).astype(BF16)


def _q_only(h, ln, wq, scale):
    n, d = h.shape
    tm = _row_tile(n, n, 512)
    row = pl.BlockSpec((tm, d), lambda i: (i, 0))
    return pl.pallas_call(
        functools.partial(_q_only_kernel, scale),
        grid=(n // tm,),
        in_specs=[row, _full((1, d)), _full(wq.shape)],
        out_specs=row,
        out_shape=jax.ShapeDtypeStruct((n, d), BF16),
        compiler_params=_cparams(("arbitrary",)),
        name="q_proj",
    )(h, ln, wq)


def kernel(x_prompt, x_sample, state_shift, state_wkv, cache_k, cache_v, cache_logf, ln1_g, ln2_g, w_gate, w_up, w_down, mu, w_r, w_k, w_v, w_o, w0, w1, w2, a0, a1, a2, g1, g2, k_k, k_a, r_k, lnx_g, lnx_b, kv_g, w_kvf, b_f, w_q, w_ob, final_g):
    d = x_prompt.shape[-1]
    nh = d // HEAD
    n_a = w_r.shape[0]
    bf = lambda z: z.astype(BF16)
    vec = lambda z: z.reshape(z.shape[:-1] + (1, d)).astype(F32)
    w = dict(
        ln1_g=vec(ln1_g), ln2_g=vec(ln2_g), w_gate=bf(w_gate), w_up=bf(w_up), w_down=bf(w_down),
        mu=mu.astype(F32), w_r=bf(w_r), w_k=bf(w_k), w_v=bf(w_v), w_o=bf(w_o),
        w0=vec(w0), w1=bf(w1), w2=bf(w2), a0=vec(a0), a1=bf(a1), a2=bf(a2), g1=bf(g1), g2=bf(g2),
        k_k=vec(k_k), k_a=vec(k_a),
        r_k=r_k.reshape(n_a, 1, d).astype(F32),
        lnx_g=vec(lnx_g), lnx_b=vec(lnx_b), kv_g=vec(kv_g),
        w_kvf_kt=bf(w_kvf[:, :d].T), w_kvf_vt=bf(w_kvf[:, d:2 * d].T), w_kvf_ft=bf(w_kvf[:, 2 * d:].T),
        b_f=b_f.reshape(nh, 1).astype(F32), w_q=bf(w_q), w_ob=bf(w_ob), final_g=vec(final_g),
    )
    bp = x_prompt.shape[0]
    dt = x_prompt.dtype
    y_p, shift_p, wkv_p, k_p, v_p, lf_p = _trunk(
        x_prompt, jnp.zeros((n_a, bp, d), dt), jnp.zeros((n_a, bp, nh, HEAD, HEAD), dt), None, w)
    y_s, shift_s, wkv_s, k_s, v_s, lf_s = _trunk(
        x_sample, state_shift, state_wkv, (cache_k, cache_v, cache_logf), w)
    return (y_p, y_s, shift_p, wkv_p, k_p, v_p, lf_p, shift_s, wkv_s, k_s, v_s, lf_s)
```

```python
import functools

import jax
import jax.numpy as jnp
from jax import lax
from jax.experimental import pallas as pl
from jax.experimental.pallas import tpu as pltpu

F32 = jnp.float32
BF16 = jnp.bfloat16

HEAD = 64
LANE_GROUP = 256
HEADS_PER_GROUP = LANE_GROUP // HEAD
CHUNK = 64
RMS_EPS = 1e-5
GN_EPS = 64e-5
LOG2E = 1.4426950408889634
NEG_EXP_M_HALF = -0.6065306597126334
AUG_LANES = 128
AUG_PIECES = 3
ONES_ROWS = 16
VMEM_LIMIT = 56 * 1024 * 1024


def _cparams(sem):
    return pltpu.CompilerParams(dimension_semantics=sem, vmem_limit_bytes=VMEM_LIMIT)


def _dot(a, b):
    return jnp.dot(a, b, preferred_element_type=F32)


def _dot_nt(a, b):
    return lax.dot_general(a, b, (((1,), (1,)), ((), ())), preferred_element_type=F32)


def _dot_tn(a, b):
    return lax.dot_general(a, b, (((0,), (0,)), ((), ())), preferred_element_type=F32)


def _split(x, parts):
    out = []
    rem = x
    for _ in range(parts):
        p = rem.astype(BF16)
        out.append(p)
        rem = rem - p.astype(F32)
    return out


def _mm(dot, a, b, pa=1, pb=1):
    a_parts = _split(a, pa)
    b_parts = _split(b, pb)
    acc = None
    for i, ap in enumerate(a_parts):
        for j, bp in enumerate(b_parts):
            if i + j >= max(pa, pb):
                continue
            t = dot(ap, bp)
            acc = t if acc is None else acc + t
    return acc


def _rms(x, g):
    ms = jnp.mean(x * x, axis=-1, keepdims=True)
    return x * lax.rsqrt(ms + RMS_EPS) * g


def _sigmoid(z):
    return 1.0 / (1.0 + jnp.exp2(z * (-LOG2E)))


def _softplus(z):
    return jnp.maximum(z, 0.0) + jnp.log(1.0 + jnp.exp(-jnp.abs(z)))


def _group_ones():
    r = lax.broadcasted_iota(jnp.int32, (LANE_GROUP, LANE_GROUP), 0) // HEAD
    c = lax.broadcasted_iota(jnp.int32, (LANE_GROUP, LANE_GROUP), 1) // HEAD
    return jnp.where(r == c, 1.0, 0.0).astype(BF16)


def _head_sum(x, ones_bd):
    outs = []
    for g in range(x.shape[1] // LANE_GROUP):
        outs.append(_dot(x[:, g * LANE_GROUP:(g + 1) * LANE_GROUP].astype(BF16), ones_bd))
    return jnp.concatenate(outs, axis=-1)


def _row_tile(n_rows, seq, cap):
    t = min(cap, seq)
    while seq % t:
        t //= 2
    assert t % 8 == 0 and n_rows % t == 0
    return t


def _full(shape):
    return pl.BlockSpec(shape, lambda *_: (0,) * len(shape))


def _rwkv_pre_kernel(tiles_per_seq, x_ref, xp_ref, sh_ref, ln_ref, mu_ref, wr_ref, wk_ref, wv_ref,
                     w1_ref, w2_ref, a1_ref, a2_ref, g1_ref, g2_ref, w0_ref, a0_ref, kk_ref, ka_ref,
                     r_out, lw_out, k_out, v_out, av_out, bv_out, g_out, xl_out):
    i = pl.program_id(0)
    ln = ln_ref[...]
    xn = _rms(x_ref[...], ln)
    tm = xn.shape[0]
    prev_tile_last = _rms(xp_ref[...], ln)[7:8, :]
    prev = jnp.where(i % tiles_per_seq == 0, sh_ref[...], prev_tile_last)
    row = lax.broadcasted_iota(jnp.int32, xn.shape, 0)
    x_prev = jnp.where(row == 0, prev, pltpu.roll(xn, 1, 0))
    xx = x_prev - xn
    xl_out[...] = xn[tm - 1:tm, :]

    sub = min(tm, 128)
    subs = [slice(s * sub, (s + 1) * sub) for s in range(tm // sub)]
    ones_bd = _group_ones()
    mixes = [[(xn[rows] + xx[rows] * mu_ref[j:j + 1, :]).astype(BF16) for j in range(6)] for rows in subs]
    r, k, v, lin = [], [], [], []
    for m in mixes:
        r.append(_dot(m[0], wr_ref[...]))
        k.append(_dot(m[2], wk_ref[...]))
        v.append(_dot(m[3], wv_ref[...]))
        lin.append((_dot(m[1], w1_ref[...]), _dot(m[4], a1_ref[...]), _dot(m[5], g1_ref[...])))
    act = [(jnp.tanh(lw).astype(BF16), la.astype(BF16), _sigmoid(lg).astype(BF16)) for lw, la, lg in lin]
    lout = [(_dot(aw, w2_ref[...]), _dot(aa, a2_ref[...]), _dot(ag, g2_ref[...])) for aw, aa, ag in act]
    kk = [ks * kk_ref[...] for ks in k]
    nsq = [_head_sum(kks * kks, ones_bd) for kks in kk]
    for s, rows in enumerate(subs):
        wl, al, g = lout[s]
        asig = _sigmoid(a0_ref[...] + al)
        kkn = kk[s] / jnp.maximum(jnp.sqrt(nsq[s]), 1e-12)
        r_out[rows, :] = r[s].astype(r_out.dtype)
        lw_out[rows, :] = NEG_EXP_M_HALF * _sigmoid(w0_ref[...] + wl)
        k_out[rows, :] = (k[s] * (1.0 + (asig - 1.0) * ka_ref[...])).astype(k_out.dtype)
        v_out[rows, :] = v[s].astype(v_out.dtype)
        av_out[rows, :] = (-kkn).astype(av_out.dtype)
        bv_out[rows, :] = (kkn * asig).astype(bv_out.dtype)
        g_out[rows, :] = g.astype(g_out.dtype)


def _rwkv_pre(x, shift0, seq, ln, mu, wr, wk, wv, w1, w2, a1, a2, g1, g2, w0, a0, k_k, k_a):
    n, d = x.shape
    b = n // seq
    tm = _row_tile(n, seq, 512)
    tps = seq // tm
    row_spec = pl.BlockSpec((tm, d), lambda i: (i, 0))
    vec = _full((1, d))
    outs = pl.pallas_call(
        functools.partial(_rwkv_pre_kernel, tps),
        grid=(n // tm,),
        in_specs=[
            row_spec,
            pl.BlockSpec((8, d), lambda i: (jnp.maximum(i * (tm // 8) - 1, 0), 0)),
            pl.BlockSpec((None, 1, d), lambda i: (i // tps, 0, 0)),
            vec, _full(mu.shape),
            _full(wr.shape), _full(wk.shape), _full(wv.shape),
            _full(w1.shape), _full(w2.shape), _full(a1.shape), _full(a2.shape),
            _full(g1.shape), _full(g2.shape),
            vec, vec, vec, vec,
        ],
        out_specs=[row_spec] * 7 + [pl.BlockSpec((None, 1, d), lambda i: (i // tps, 0, 0))],
        out_shape=[jax.ShapeDtypeStruct((n, d), F32 if j == 1 else BF16) for j in range(7)]
        + [jax.ShapeDtypeStruct((b, 1, d), F32)],
        compiler_params=_cparams(("arbitrary",)),
        name="rwkv_pre",
    )(x, x, shift0, ln, mu, wr, wk, wv, w1, w2, a1, a2, g1, g2, w0, a0, k_k, k_a)
    return outs


def _scan_kernel(n_chunks, r_ref, lw_ref, k_ref, v_ref, a_ref, b_ref, s0_ref, y_ref, so_ref, s_scr):
    c = pl.program_id(1)
    C = CHUNK
    G = LANE_GROUP
    ri = lax.broadcasted_iota(jnp.int32, (G, G), 0)
    ci = lax.broadcasted_iota(jnp.int32, (G, G), 1)
    bd = (ri // HEAD) == (ci // HEAD)

    def expand(x):
        return jnp.where(bd, jnp.concatenate([x] * HEADS_PER_GROUP, axis=0), 0.0)

    n_seq, n_grp = s0_ref.shape[0], s0_ref.shape[1]
    chains = [(s, g) for s in range(n_seq) for g in range(n_grp)]
    n_groups = len(chains)

    def load_state():
        for i, (s, g) in enumerate(chains):
            s_scr[i] = expand(s0_ref[s, g])

    if n_chunks == 1:
        load_state()
    else:
        pl.when(c == 0)(load_state)

    tpos = lax.broadcasted_iota(jnp.int32, (C, G), 0)
    spos = lax.broadcasted_iota(jnp.int32, (C, G), 1) % C
    strict = spos < tpos
    incl = spos <= tpos
    tri = jnp.where(lax.broadcasted_iota(jnp.int32, (C, C), 0) >= lax.broadcasted_iota(jnp.int32, (C, C), 1),
                    1.0, 0.0).astype(BF16)
    steps = C.bit_length() - 1

    groups = range(n_groups)
    sls = [(s, slice(None), slice(g * G, (g + 1) * G)) for s, g in chains]
    lw = [lw_ref[sl] for sl in sls]
    v = [v_ref[sl] for sl in sls]
    cs = [_mm(_dot, tri, lw[g], 1, 2) for g in groups]
    p_in = [jnp.exp(cs[g]) for g in groups]
    p_inv = [jnp.exp(-cs[g]) for g in groups]
    p_end = [p_in[g][C - 1:C, :] for g in groups]
    rt = [r_ref[sls[g]] * p_in[g] for g in groups]
    at = [a_ref[sls[g]] * jnp.exp(cs[g] - lw[g]) for g in groups]
    bt = [b_ref[sls[g]] * p_inv[g] for g in groups]
    kt = [k_ref[sls[g]] * p_inv[g] for g in groups]
    S = [s_scr[g] for g in groups]
    la = [jnp.concatenate([at[g], rt[g]], axis=0) for g in groups]
    ab = [_mm(_dot_nt, la[g], expand(bt[g])) for g in groups]
    ak = [_mm(_dot_nt, la[g], expand(kt[g])) for g in groups]
    npow = [jnp.where(strict, ab[g][:C], 0.0) for g in groups]
    a_rb = [jnp.where(incl, ab[g][C:], 0.0) for g in groups]
    a_k = [jnp.where(jnp.concatenate([strict, incl], axis=0), ak[g], 0.0) for g in groups]
    from_v = [_mm(_dot, a_k[g], expand(v[g])) for g in groups]
    from_s = [_mm(_dot_nt, la[g], S[g]) for g in groups]
    w0 = [from_s[g][:C] + from_v[g][:C] for g in groups]
    y0 = [from_s[g][C:] + from_v[g][C:] for g in groups]
    eye = jnp.where(spos == tpos, 1.0, 0.0)
    tinv = [eye + npow[g] for g in groups]
    npow = [_mm(_dot, npow[g], expand(npow[g])) for g in groups]
    for i in range(1, steps):
        last = i + 1 == steps
        lhs = [tinv[g] if last else jnp.concatenate([tinv[g], npow[g]], axis=0) for g in groups]
        prod = [_mm(_dot, lhs[g], expand(npow[g])) for g in groups]
        tinv = [tinv[g] + prod[g][:C] for g in groups]
        if not last:
            npow = [prod[g][C:] for g in groups]
    u = [_mm(_dot, tinv[g], expand(w0[g])) for g in groups]
    for g in groups:
        y_ref[sls[g]] = y0[g] + _mm(_dot, a_rb[g], expand(u[g]))
    for g in groups:
        upd = _mm(_dot_tn, jnp.concatenate([u[g], v[g]], axis=0),
                  jnp.concatenate([bt[g], kt[g]], axis=0) * p_end[g])
        s_scr[g] = S[g] * p_end[g] + jnp.where(bd, upd, 0.0)

    def write_state():
        for i, (s, g) in enumerate(chains):
            acc = s_scr[i, 0:HEAD, :]
            for h in range(1, HEADS_PER_GROUP):
                acc = acc + s_scr[i, h * HEAD:(h + 1) * HEAD, :]
            so_ref[s, g] = acc

    if n_chunks == 1:
        write_state()
    else:
        pl.when(c == n_chunks - 1)(write_state)


def _wkv_scan(r, lw, k, v, av, bv, s0c, seq):
    n, d = r.shape
    b = n // seq
    ng = d // LANE_GROUP
    nc = seq // CHUNK
    sps = 4 if b % 4 == 0 else (2 if b % 2 == 0 else 1)
    blk = pl.BlockSpec((sps, CHUNK, d), lambda bi, ci: (bi, ci, 0))
    st = pl.BlockSpec((sps, ng, HEAD, LANE_GROUP), lambda bi, ci: (bi, 0, 0, 0))
    y, s_out = pl.pallas_call(
        functools.partial(_scan_kernel, nc),
        grid=(b // sps, nc),
        in_specs=[blk] * 6 + [st],
        out_specs=[blk, st],
        out_shape=[jax.ShapeDtypeStruct((b, seq, d), F32), jax.ShapeDtypeStruct(s0c.shape, F32)],
        scratch_shapes=[pltpu.VMEM((sps * ng, LANE_GROUP, LANE_GROUP), F32)],
        compiler_params=_cparams(("arbitrary", "arbitrary")),
        name="wkv_scan",
    )(*(z.reshape(b, seq, d) for z in (r, lw, k, v, av, bv)), s0c)
    return y.reshape(n, d), s_out


def _rwkv_post_kernel(y_ref, r_ref, k_ref, v_ref, g_ref, h_ref, lg_ref, lb_ref, rk_ref, wo_ref, o_ref):
    ones_bd = _group_ones()
    tm = y_ref.shape[0]
    sub = min(tm, 128)
    subs = [slice(s * sub, (s + 1) * sub) for s in range(tm // sub)]
    y = [y_ref[rows, :] for rows in subs]
    mean = [_head_sum(ys, ones_bd) * (1.0 / HEAD) for ys in y]
    rk = [_head_sum(r_ref[rows, :].astype(F32) * k_ref[rows, :].astype(F32) * rk_ref[...], ones_bd) for rows in subs]
    yc = [y[s] - mean[s] for s in range(len(subs))]
    var = [_head_sum(c * c, ones_bd) * (1.0 / HEAD) for c in yc]
    for s, rows in enumerate(subs):
        yn = yc[s] * lax.rsqrt(var[s] + GN_EPS) * lg_ref[...] + lb_ref[...]
        yn = yn + rk[s] * v_ref[rows, :].astype(F32)
        gated = (yn * g_ref[rows, :].astype(F32)).astype(BF16)
        o_ref[rows, :] = h_ref[rows, :] + _dot(gated, wo_ref[...])


def _rwkv_post(y, r, k, v, g, h, lnx_g, lnx_b, r_k, wo):
    n, d = y.shape
    tm = _row_tile(n, n, 512)
    row = pl.BlockSpec((tm, d), lambda i: (i, 0))
    vec = _full((1, d))
    return pl.pallas_call(
        _rwkv_post_kernel,
        grid=(n // tm,),
        in_specs=[row] * 6 + [vec, vec, vec, _full(wo.shape)],
        out_specs=row,
        out_shape=jax.ShapeDtypeStruct((n, d), F32),
        compiler_params=_cparams(("arbitrary",)),
        name="rwkv_post",
    )(y, r, k, v, g, h, lnx_g, lnx_b, r_k, wo)


def _ffn_kernel(final, proj, tf, h_ref, ln_ref, wg_ref, wu_ref, wd_ref, *rest):
    rest = list(rest)
    a_ref, wa_ref = (rest.pop(0), rest.pop(0)) if proj else (None, None)
    fg_ref = rest.pop(0) if final else None
    o_ref = rest.pop(0)
    y_ref = rest.pop(0) if final else None
    act_scr, = rest
    h = h_ref[...]
    if proj:
        h = h + _dot(a_ref[...], wa_ref[...])
    xn = _rms(h, ln_ref[...]).astype(BF16)
    for c in range(act_scr.shape[1] // tf):
        cols = slice(c * tf, (c + 1) * tf)
        gate = _dot(xn, wg_ref[:, cols])
        up = _dot(xn, wu_ref[:, cols])
        act_scr[:, cols] = (gate * _sigmoid(gate) * up).astype(BF16)
    out = h + _dot(act_scr[...], wd_ref[...])
    o_ref[...] = out
    if final:
        y_ref[...] = _rms(out, fg_ref[...])


def _ffn(h, ln, wg, wu, wd, final_g=None, proj=None):
    n, d = h.shape
    dff = wg.shape[1]
    tm = _row_tile(n, n, 512 if proj is not None else 1024)
    tf = 256
    assert dff % tf == 0
    final = final_g is not None
    row = pl.BlockSpec((tm, d), lambda i: (i, 0))
    vec = pl.BlockSpec((1, d), lambda i: (0, 0))
    once = pl.Buffered(1)
    in_specs = [row, vec,
                pl.BlockSpec((d, dff), lambda i: (0, 0), pipeline_mode=once),
                pl.BlockSpec((d, dff), lambda i: (0, 0), pipeline_mode=once),
                pl.BlockSpec((dff, d), lambda i: (0, 0), pipeline_mode=once)]
    args = [h, ln, wg, wu, wd]
    out_specs = [row]
    out_shape = [jax.ShapeDtypeStruct((n, d), F32)]
    if proj is not None:
        in_specs += [row, pl.BlockSpec((d, d), lambda i: (0, 0), pipeline_mode=once)]
        args += list(proj)
    if final:
        in_specs.append(vec)
        args.append(final_g)
        out_specs.append(row)
        out_shape.append(jax.ShapeDtypeStruct((n, d), F32))
    res = pl.pallas_call(
        functools.partial(_ffn_kernel, final, proj is not None, tf),
        grid=(n // tm,),
        in_specs=in_specs,
        out_specs=out_specs,
        out_shape=out_shape,
        scratch_shapes=[pltpu.VMEM((tm, dff), BF16)],
        compiler_params=_cparams(("arbitrary",)),
        name="ffn_final" if final else "ffn",
    )(*args)
    return res


def _kvq_kernel(scale, h_ref, kvg_ref, ln_ref, wkt_ref, wvt_ref, wft_ref, bf_ref, wq_ref,
                kt_out, vt_out, f_out, q_out):
    h = h_ref[...]
    ms = jnp.mean(h * h, axis=-1, keepdims=True)
    hn = h * lax.rsqrt(ms + RMS_EPS)
    xkv = (hn * kvg_ref[...]).astype(BF16)
    xq = (hn * ln_ref[...]).astype(BF16)
    kt_out[...] = _dot_nt(wkt_ref[...], xkv)
    vt_out[...] = _dot_nt(wvt_ref[...], xkv)
    z = _dot_nt(wft_ref[...], xkv) + bf_ref[...]
    f_out[...] = -_softplus(-z)
    q_out[...] = (_dot(xq, wq_ref[...]) * scale).astype(BF16)


def _kvq(h, seq, kv_g, ln, wkt, wvt, wft, b_f, wq, scale):
    n, d = h.shape
    b = n // seq
    nh = wft.shape[0]
    tm = _row_tile(n, seq, 512)
    tps = seq // tm
    row = pl.BlockSpec((tm, d), lambda i: (i, 0))
    vec = _full((1, d))
    col = pl.BlockSpec((None, d, tm), lambda i: (i // tps, 0, i % tps))
    return pl.pallas_call(
        functools.partial(_kvq_kernel, scale),
        grid=(n // tm,),
        in_specs=[row, vec, vec, _full(wkt.shape), _full(wvt.shape), _full(wft.shape), _full((nh, 1)), _full(wq.shape)],
        out_specs=[col, col, pl.BlockSpec((None, nh, tm), lambda i: (i // tps, 0, i % tps)), row],
        out_shape=[jax.ShapeDtypeStruct((b, d, seq), F32), jax.ShapeDtypeStruct((b, d, seq), F32),
                   jax.ShapeDtypeStruct((b, nh, seq), F32), jax.ShapeDtypeStruct((n, d), BF16)],
        compiler_params=_cparams(("arbitrary",)),
        name="kvq_proj",
    )(h, kv_g, ln, wkt, wvt, wft, b_f, wq)


def _cumsum_kernel(blk, f_ref, c0_ref, row_ref, *rest):
    aug_ref, last_ref = rest if len(rest) == 2 else (None, rest[0])
    nh, t = f_ref.shape
    triu = jnp.where(lax.broadcasted_iota(jnp.int32, (blk, blk), 0) <= lax.broadcasted_iota(jnp.int32, (blk, blk), 1),
                     1.0, 0.0).astype(BF16)
    head = lax.broadcasted_iota(jnp.int32, (nh, AUG_LANES), 0)
    lane = lax.broadcasted_iota(jnp.int32, (nh, AUG_LANES), 1)
    place = [jnp.where(lane == AUG_PIECES * head + i, 1.0, 0.0).astype(BF16) for i in range(AUG_PIECES)]
    carry = c0_ref[...]
    for j in range(t // blk):
        cols = slice(j * blk, (j + 1) * blk)
        cb = _mm(_dot, f_ref[:, cols], triu, 3, 1) + carry
        row_ref[:, cols] = cb
        if aug_ref is not None:
            pieces = _split(cb * LOG2E, AUG_PIECES)
            aug = _dot_tn(pieces[0], place[0])
            for i in range(1, AUG_PIECES):
                aug = aug + _dot_tn(pieces[i], place[i])
            aug_ref[cols, :] = aug.astype(BF16)
        carry = cb[:, blk - 1:blk]
    last_ref[...] = carry


def _cumsum(logf, c0, with_aug):
    b, nh, t = logf.shape
    blk = min(t, 256)
    assert t % blk == 0 and AUG_PIECES * nh <= AUG_LANES
    row = pl.BlockSpec((None, nh, t), lambda i: (i, 0, 0))
    one = pl.BlockSpec((None, nh, 1), lambda i: (i, 0, 0))
    out_specs = [row, one]
    out_shape = [jax.ShapeDtypeStruct((b, nh, t), F32), jax.ShapeDtypeStruct((b, nh, 1), F32)]
    if with_aug:
        out_specs.insert(1, pl.BlockSpec((None, t, AUG_LANES), lambda i: (i, 0, 0)))
        out_shape.insert(1, jax.ShapeDtypeStruct((b, t, AUG_LANES), BF16))
    return pl.pallas_call(
        functools.partial(_cumsum_kernel, blk),
        grid=(b,),
        in_specs=[row, one],
        out_specs=out_specs,
        out_shape=out_shape,
        compiler_params=_cparams(("arbitrary",)),
        name="logf_cumsum",
    )(logf, c0)


def _head_pair_masks(shape):
    lane = lax.broadcasted_iota(jnp.int32, shape, len(shape) - 1)
    return lane < HEAD


def _attn_prompt_kernel(tk, q_ref, k_ref, v_ref, aug_ref, o_ref, ka_scr, vt_scr, st_scr, pm_scr, pt_scr):
    pi = pl.program_id(1)
    seq = q_ref.shape[0]
    nk = seq // tk
    vrows = HEAD + ONES_ROWS

    for kb in range(nk):
        ka_scr[kb * tk:(kb + 1) * tk, 0:2 * HEAD] = k_ref[:, kb * tk:(kb + 1) * tk].T.astype(BF16)
    ka_scr[:, 2 * HEAD:] = aug_ref[...]
    for h in range(2):
        vt_scr[h, 0:HEAD, :] = v_ref[h * HEAD:(h + 1) * HEAD, :].astype(BF16)
        vt_scr[h, HEAD:vrows, :] = jnp.ones((ONES_ROWS, seq), BF16)

    lane = lax.broadcasted_iota(jnp.int32, (tk, AUG_LANES), 1)
    firstq = _head_pair_masks((tk, 2 * HEAD))
    minus = []
    for h in range(2):
        lo = AUG_PIECES * (2 * pi + h)
        minus.append(jnp.where((lane >= lo) & (lane < lo + AUG_PIECES), -1.0, 0.0).astype(BF16))
    kpos = lax.broadcasted_iota(jnp.int32, (tk, tk), 0)
    qpos = lax.broadcasted_iota(jnp.int32, (tk, tk), 1)
    causal = kpos <= qpos

    def score_tiles(j):
        slot = j % 2
        q = q_ref[j * tk:(j + 1) * tk, :]
        zero = jnp.zeros_like(q)
        for h in range(2):
            qh = jnp.where(firstq, q, zero) if h == 0 else jnp.where(firstq, zero, q)
            qa = jnp.concatenate([qh, minus[h]], axis=1)
            pm = None
            for kb in range(j + 1):
                s = _dot_nt(ka_scr[kb * tk:(kb + 1) * tk, :], qa)
                if kb == j:
                    s = jnp.where(causal, s, -jnp.inf)
                st_scr[slot, h, kb * tk:(kb + 1) * tk, :] = s
                part = jnp.max(s.reshape(tk // 8, 8, tk), axis=0)
                pm = part if pm is None else jnp.maximum(pm, part)
                yield
            pm_scr[slot, h] = pm

    def prob_tiles(j):
        slot = j % 2
        for h in range(2):
            mx = jnp.max(pm_scr[slot, h], axis=0, keepdims=True)
            for kb in range(j + 1):
                rows = slice(kb * tk, (kb + 1) * tk)
                pt_scr[slot, h, rows, :] = jnp.exp2(st_scr[slot, h, rows, :] - mx).astype(BF16)
                yield

    def finish(j):
        slot = j % 2
        outs = []
        for h in range(2):
            acc = _dot(vt_scr[h, :, 0:(j + 1) * tk], pt_scr[slot, h, 0:(j + 1) * tk, :])
            outs.append(acc[0:HEAD] / acc[HEAD:HEAD + 1])
        o_ref[j * tk:(j + 1) * tk, :] = jnp.concatenate(outs, axis=0).T.astype(o_ref.dtype)

    def run(*gens):
        gens = list(gens)
        while gens:
            for gen in list(gens):
                if next(gen, StopIteration) is StopIteration:
                    gens.remove(gen)

    run(score_tiles(0))
    run(*([score_tiles(1)] if nk > 1 else []), prob_tiles(0))
    for j in range(nk):
        stage = []
        if j + 2 < nk:
            stage.append(score_tiles(j + 2))
        if j + 1 < nk:
            stage.append(prob_tiles(j + 1))
        run(*stage)
        finish(j)


def _attn_prompt(q, kt, vt, c_aug, seq):
    n, d = q.shape
    b = n // seq
    npair = d // (2 * HEAD)
    tk = min(seq, 256)
    assert seq % tk == 0
    blk = pl.BlockSpec((seq, 2 * HEAD), lambda bi, pi: (bi, pi))
    blk_t = pl.BlockSpec((None, 2 * HEAD, seq), lambda bi, pi: (bi, pi, 0))
    return pl.pallas_call(
        functools.partial(_attn_prompt_kernel, tk),
        grid=(b, npair),
        in_specs=[blk, blk_t, blk_t, pl.BlockSpec((None, seq, AUG_LANES), lambda bi, pi: (bi, 0, 0))],
        out_specs=blk,
        out_shape=jax.ShapeDtypeStruct((n, d), BF16),
        scratch_shapes=[pltpu.VMEM((seq, 2 * HEAD + AUG_LANES), BF16),
                        pltpu.VMEM((2, HEAD + ONES_ROWS, seq), BF16),
                        pltpu.VMEM((2, 2, seq, tk), F32),
                        pltpu.VMEM((2, 2, 8, tk), F32),
                        pltpu.VMEM((2, 2, seq, tk), BF16)],
        compiler_params=_cparams(("arbitrary", "arbitrary")),
        name="fox_attn_prompt",
    )(q, kt, vt, c_aug)


def _attn_sample_kernel(q_ref, kp_ref, vp_ref, kn_ref, vn_ref, ckp_ref, ckn_ref, o_ref):
    pw = 2 * HEAD
    heads = [(p, h) for p in range(q_ref.shape[1] // pw) for h in range(2)]
    first = _head_pair_masks((q_ref.shape[0], pw))
    qpos = lax.broadcasted_iota(jnp.int32, (kn_ref.shape[1], kn_ref.shape[1]), 0)
    kpos = lax.broadcasted_iota(jnp.int32, (kn_ref.shape[1], kn_ref.shape[1]), 1)
    sp, sn = {}, {}
    for p, h in heads:
        q = q_ref[:, p * pw:(p + 1) * pw]
        qh = jnp.where(first, q, jnp.zeros_like(q)) if h == 0 else jnp.where(first, jnp.zeros_like(q), q)
        kp = kp_ref[p * pw:(p + 1) * pw, :].astype(BF16)
        kn = kn_ref[p * pw:(p + 1) * pw, :].astype(BF16)
        sp[p, h] = _dot(qh, kp) - ckp_ref[2 * p + h:2 * p + h + 1, :] * LOG2E
        sn[p, h] = jnp.where(kpos <= qpos, _dot(qh, kn) - ckn_ref[2 * p + h:2 * p + h + 1, :] * LOG2E, -jnp.inf)
    pp, pn, ls = {}, {}, {}
    for c in heads:
        m = jnp.maximum(jnp.max(sp[c], axis=-1, keepdims=True), jnp.max(sn[c], axis=-1, keepdims=True))
        ep = jnp.exp2(sp[c] - m)
        en = jnp.exp2(sn[c] - m)
        ls[c] = jnp.sum(ep, axis=-1, keepdims=True) + jnp.sum(en, axis=-1, keepdims=True)
        pp[c] = ep.astype(BF16)
        pn[c] = en.astype(BF16)
    for p in range(q_ref.shape[1] // pw):
        vp = vp_ref[p * pw:(p + 1) * pw, :].astype(BF16)
        vn = vn_ref[p * pw:(p + 1) * pw, :].astype(BF16)
        outs = [(_dot_nt(pp[p, h], vp) + _dot_nt(pn[p, h], vn)) / ls[p, h] for h in range(2)]
        o_ref[:, p * pw:(p + 1) * pw] = jnp.where(first, outs[0], outs[1]).astype(o_ref.dtype)


def _attn_sample(q, kpt, vpt, knt, vnt, ckp_row, ckn_row, seq, past):
    n, d = q.shape
    b = n // seq
    npair = d // (2 * HEAD)
    pps = 4 if npair % 4 == 0 else 1
    ng = npair // pps
    ckp = ckp_row.reshape(b, ng, 2 * pps, past)
    ckn = ckn_row.reshape(b, ng, 2 * pps, seq)
    rows = pl.BlockSpec((seq, pps * 2 * HEAD), lambda bi, pi: (bi, pi))
    new = pl.BlockSpec((None, pps * 2 * HEAD, seq), lambda bi, pi: (bi, pi, 0))
    old = pl.BlockSpec((None, pps * 2 * HEAD, past), lambda bi, pi: (bi, pi, 0))
    return pl.pallas_call(
        _attn_sample_kernel,
        grid=(b, ng),
        in_specs=[rows, old, old, new, new,
                  pl.BlockSpec((None, None, 2 * pps, past), lambda bi, pi: (bi, pi, 0, 0)),
                  pl.BlockSpec((None, None, 2 * pps, seq), lambda bi, pi: (bi, pi, 0, 0))],
        out_specs=rows,
        out_shape=jax.ShapeDtypeStruct((n, d), BF16),
        compiler_params=_cparams(("arbitrary", "arbitrary")),
        name="fox_attn_sample",
    )(q, kpt, vpt, knt, vnt, ckp, ckn)


def _pack_state(s):
    b, h, v, k = s.shape
    return s.reshape(b, h // HEADS_PER_GROUP, HEADS_PER_GROUP, v, k).transpose(0, 1, 3, 2, 4).reshape(
        b, h // HEADS_PER_GROUP, v, HEADS_PER_GROUP * k)


def _unpack_state(sc):
    b, g, v, gk = sc.shape
    k = gk // HEADS_PER_GROUP
    return sc.reshape(b, g, v, HEADS_PER_GROUP, k).transpose(0, 1, 3, 2, 4).reshape(b, g * HEADS_PER_GROUP, v, k)


def _trunk(x, shift0, wkv0, past, w):
    b, t, d = x.shape
    n = b * t
    h = x.reshape(n, d)
    n_a = w["w_r"].shape[0]
    new_shift, new_wkv = [], []
    for l in range(n_a):
        r, lw, k, v, av, bv, g, xl = _rwkv_pre(
            h, shift0[l].reshape(b, 1, d), t, w["ln1_g"][l], w["mu"][l], w["w_r"][l], w["w_k"][l], w["w_v"][l],
            w["w1"][l], w["w2"][l], w["a1"][l], w["a2"][l], w["g1"][l], w["g2"][l],
            w["w0"][l], w["a0"][l], w["k_k"][l], w["k_a"][l])
        tp = -(-t // CHUNK) * CHUNK
        scan_in = (r, lw, k, v, av, bv)
        if tp != t:
            scan_in = tuple(jnp.pad(z.reshape(b, t, d), ((0, 0), (0, tp - t), (0, 0))).reshape(b * tp, d)
                            for z in scan_in)
        y, s_out = _wkv_scan(*scan_in, _pack_state(wkv0[l]), tp)
        if tp != t:
            y = y.reshape(b, tp, d)[:, :t].reshape(n, d)
        h = _rwkv_post(y, r, k, v, g, h, w["lnx_g"][l], w["lnx_b"][l], w["r_k"][l], w["w_o"][l])
        h = _ffn(h, w["ln2_g"][l], w["w_gate"][l], w["w_up"][l], w["w_down"][l])[0]
        new_shift.append(xl.reshape(b, d))
        new_wkv.append(_unpack_state(s_out))

    nh = d // HEAD
    n_b = w["w_q"].shape[0]
    scale = HEAD ** -0.5 * LOG2E
    kt_new, vt_new, logf, q = _kvq(h, t, w["kv_g"], w["ln1_g"][n_a], w["w_kvf_kt"], w["w_kvf_vt"], w["w_kvf_ft"],
                                   w["b_f"], w["w_q"][0], scale)
    zero_carry = jnp.zeros((b, nh, 1), F32)
    if past is None:
        _, c_aug, _ = _cumsum(logf, zero_carry, True)
    else:
        pk, pv, plf = past
        p = pk.shape[1]
        pkt = jnp.transpose(pk, (0, 2, 3, 1)).reshape(b, d, p).astype(F32)
        pvt = jnp.transpose(pv, (0, 2, 3, 1)).reshape(b, d, p).astype(F32)
        cp_row, cp_last = _cumsum(jnp.transpose(plf, (0, 2, 1)).astype(F32), zero_carry, False)
        c_row, _ = _cumsum(logf, cp_last, False)
    for j in range(n_b):
        l = n_a + j
        if j > 0:
            q = _q_only(h, w["ln1_g"][l], w["w_q"][j], scale)
        if past is None:
            o = _attn_prompt(q, kt_new, vt_new, c_aug, t)
        else:
            o = _attn_sample(q, pkt, pvt, kt_new, vt_new, cp_row, c_row, t, p)
        last = j == n_b - 1
        res = _ffn(h, w["ln2_g"][l], w["w_gate"][l], w["w_up"][l], w["w_down"][l],
                   w["final_g"] if last else None, proj=(o, w["w_ob"][j]))
        h = res[0]
        if last:
            y_out = res[1]
    k_new = jnp.transpose(kt_new.reshape(b, nh, HEAD, t), (0, 3, 1, 2))
    v_new = jnp.transpose(vt_new.reshape(b, nh, HEAD, t), (0, 3, 1, 2))
    return (y_out.reshape(b, t, d), jnp.stack(new_shift), jnp.stack(new_wkv),
            k_new, v_new, jnp.transpose(logf, (0, 2, 1)))


def _q_only_kernel(scale, h_ref, ln_ref, wq_ref, q_out):
    q_out[...] = (_dot(_rms(h_ref[...], ln_ref[...]).astype(BF16), wq_ref[...]) * scale).astype(BF16)


def _q_only(h, ln, wq, scale):
    n, d = h.shape
    tm = _row_tile(n, n, 512)
    row = pl.BlockSpec((tm, d), lambda i: (i, 0))
    return pl.pallas_call(
        functools.partial(_q_only_kernel, scale),
        grid=(n // tm,),
        in_specs=[row, _full((1, d)), _full(wq.shape)],
        out_specs=row,
        out_shape=jax.ShapeDtypeStruct((n, d), BF16),
        compiler_params=_cparams(("arbitrary",)),
        name="q_proj",
    )(h, ln, wq)


def kernel(x_prompt, x_sample, state_shift, state_wkv, cache_k, cache_v, cache_logf, ln1_g, ln2_g, w_gate, w_up, w_down, mu, w_r, w_k, w_v, w_o, w0, w1, w2, a0, a1, a2, g1, g2, k_k, k_a, r_k, lnx_g, lnx_b, kv_g, w_kvf, b_f, w_q, w_ob, final_g):
    d = x_prompt.shape[-1]
    nh = d // HEAD
    n_a = w_r.shape[0]
    bf = lambda z: z.astype(BF16)
    vec = lambda z: z.reshape(z.shape[:-1] + (1, d)).astype(F32)
    w = dict(
        ln1_g=vec(ln1_g), ln2_g=vec(ln2_g), w_gate=bf(w_gate), w_up=bf(w_up), w_down=bf(w_down),
        mu=mu.astype(F32), w_r=bf(w_r), w_k=bf(w_k), w_v=bf(w_v), w_o=bf(w_o),
        w0=vec(w0), w1=bf(w1), w2=bf(w2), a0=vec(a0), a1=bf(a1), a2=bf(a2), g1=bf(g1), g2=bf(g2),
        k_k=vec(k_k), k_a=vec(k_a),
        r_k=r_k.reshape(n_a, 1, d).astype(F32),
        lnx_g=vec(lnx_g), lnx_b=vec(lnx_b), kv_g=vec(kv_g),
        w_kvf_kt=bf(w_kvf[:, :d].T), w_kvf_vt=bf(w_kvf[:, d:2 * d].T), w_kvf_ft=bf(w_kvf[:, 2 * d:].T),
        b_f=b_f.reshape(nh, 1).astype(F32), w_q=bf(w_q), w_ob=bf(w_ob), final_g=vec(final_g),
    )
    bp = x_prompt.shape[0]
    dt = x_prompt.dtype
    y_p, shift_p, wkv_p, k_p, v_p, lf_p = _trunk(
        x_prompt, jnp.zeros((n_a, bp, d), dt), jnp.zeros((n_a, bp, nh, HEAD, HEAD), dt), None, w)
    y_s, shift_s, wkv_s, k_s, v_s, lf_s = _trunk(
        x_sample, state_shift, state_wkv, (cache_k, cache_v, cache_logf), w)
    return (y_p, y_s, shift_p, wkv_p, k_p, v_p, lf_p, shift_s, wkv_s, k_s, v_s, lf_s)
```

```python
import functools

import jax
import jax.numpy as jnp
from jax import lax
from jax.experimental import pallas as pl
from jax.experimental.pallas import tpu as pltpu

F32 = jnp.float32
BF16 = jnp.bfloat16

HEAD = 64
LANE_GROUP = 256
HEADS_PER_GROUP = LANE_GROUP // HEAD
CHUNK = 64
RMS_EPS = 1e-5
GN_EPS = 64e-5
LOG2E = 1.4426950408889634
NEG_EXP_M_HALF = -0.6065306597126334
AUG_LANES = 128
AUG_PIECES = 3
ONES_ROWS = 16
VMEM_LIMIT = 56 * 1024 * 1024
ROWS_MIX = 512
ROWS_WIDE = 1024
SUB_ROWS = 128


def _cparams(sem):
    return pltpu.CompilerParams(dimension_semantics=sem, vmem_limit_bytes=VMEM_LIMIT)


def _dot(a, b):
    return jnp.dot(a, b, preferred_element_type=F32)


def _dot_nt(a, b):
    return lax.dot_general(a, b, (((1,), (1,)), ((), ())), preferred_element_type=F32)


def _dot_tn(a, b):
    return lax.dot_general(a, b, (((0,), (0,)), ((), ())), preferred_element_type=F32)


def _split(x, parts):
    out = []
    rem = x
    for _ in range(parts):
        p = rem.astype(BF16)
        out.append(p)
        rem = rem - p.astype(F32)
    return out


def _mm(dot, a, b, pa=1, pb=1):
    a_parts = _split(a, pa)
    b_parts = _split(b, pb)
    acc = None
    for i, ap in enumerate(a_parts):
        for j, bp in enumerate(b_parts):
            if i + j >= max(pa, pb):
                continue
            t = dot(ap, bp)
            acc = t if acc is None else acc + t
    return acc


def _rms(x, g):
    ms = jnp.mean(x * x, axis=-1, keepdims=True)
    return x * lax.rsqrt(ms + RMS_EPS) * g


def _sigmoid(z):
    return 1.0 / (1.0 + jnp.exp2(z * (-LOG2E)))


def _softplus(z):
    return jnp.maximum(z, 0.0) + jnp.log(1.0 + jnp.exp(-jnp.abs(z)))


def _group_ones():
    r = lax.broadcasted_iota(jnp.int32, (LANE_GROUP, LANE_GROUP), 0) // HEAD
    c = lax.broadcasted_iota(jnp.int32, (LANE_GROUP, LANE_GROUP), 1) // HEAD
    return jnp.where(r == c, 1.0, 0.0).astype(BF16)


def _head_sum(x, ones_bd):
    outs = []
    for g in range(x.shape[1] // LANE_GROUP):
        outs.append(_dot(x[:, g * LANE_GROUP:(g + 1) * LANE_GROUP].astype(BF16), ones_bd))
    return jnp.concatenate(outs, axis=-1)


def _row_tile(n_rows, seq, cap):
    t = min(cap, seq)
    while seq % t:
        t //= 2
    assert t % 8 == 0 and n_rows % t == 0
    return t


def _full(shape):
    return pl.BlockSpec(shape, lambda *_: (0,) * len(shape))


def _rwkv_pre_kernel(tiles_per_seq, x_ref, xp_ref, sh_ref, ln_ref, mu_ref, wr_ref, wk_ref, wv_ref,
                     w1_ref, w2_ref, a1_ref, a2_ref, g1_ref, g2_ref, w0_ref, a0_ref, kk_ref, ka_ref,
                     r_out, lw_out, k_out, v_out, av_out, bv_out, g_out, xl_out):
    i = pl.program_id(0)
    ln = ln_ref[...]
    xn = _rms(x_ref[...], ln)
    tm = xn.shape[0]
    prev_tile_last = _rms(xp_ref[...], ln)[7:8, :]
    prev = jnp.where(i % tiles_per_seq == 0, sh_ref[...], prev_tile_last)
    row = lax.broadcasted_iota(jnp.int32, xn.shape, 0)
    x_prev = jnp.where(row == 0, prev, pltpu.roll(xn, 1, 0))
    xx = x_prev - xn
    xl_out[...] = xn[tm - 1:tm, :]

    sub = min(tm, SUB_ROWS)
    subs = [slice(s * sub, (s + 1) * sub) for s in range(tm // sub)]
    ones_bd = _group_ones()
    mixes = [[(xn[rows] + xx[rows] * mu_ref[j:j + 1, :]).astype(BF16) for j in range(6)] for rows in subs]
    r, k, v, lin = [], [], [], []
    for m in mixes:
        r.append(_dot(m[0], wr_ref[...]))
        k.append(_dot(m[2], wk_ref[...]))
        v.append(_dot(m[3], wv_ref[...]))
        lin.append((_dot(m[1], w1_ref[...]), _dot(m[4], a1_ref[...]), _dot(m[5], g1_ref[...])))
    act = [(jnp.tanh(lw).astype(BF16), la.astype(BF16), _sigmoid(lg).astype(BF16)) for lw, la, lg in lin]
    lout = [(_dot(aw, w2_ref[...]), _dot(aa, a2_ref[...]), _dot(ag, g2_ref[...])) for aw, aa, ag in act]
    kk = [ks * kk_ref[...] for ks in k]
    nsq = [_head_sum(kks * kks, ones_bd) for kks in kk]
    for s, rows in enumerate(subs):
        wl, al, g = lout[s]
        asig = _sigmoid(a0_ref[...] + al)
        kkn = kk[s] / jnp.maximum(jnp.sqrt(nsq[s]), 1e-12)
        r_out[rows, :] = r[s].astype(r_out.dtype)
        lw_out[rows, :] = NEG_EXP_M_HALF * _sigmoid(w0_ref[...] + wl)
        k_out[rows, :] = (k[s] * (1.0 + (asig - 1.0) * ka_ref[...])).astype(k_out.dtype)
        v_out[rows, :] = v[s].astype(v_out.dtype)
        av_out[rows, :] = (-kkn).astype(av_out.dtype)
        bv_out[rows, :] = (kkn * asig).astype(bv_out.dtype)
        g_out[rows, :] = g.astype(g_out.dtype)


def _rwkv_pre(x, shift0, seq, ln, mu, wr, wk, wv, w1, w2, a1, a2, g1, g2, w0, a0, k_k, k_a):
    n, d = x.shape
    b = n // seq
    tm = _row_tile(n, seq, ROWS_MIX)
    tps = seq // tm
    row_spec = pl.BlockSpec((tm, d), lambda i: (i, 0))
    vec = _full((1, d))
    outs = pl.pallas_call(
        functools.partial(_rwkv_pre_kernel, tps),
        grid=(n // tm,),
        in_specs=[
            row_spec,
            pl.BlockSpec((8, d), lambda i: (jnp.maximum(i * (tm // 8) - 1, 0), 0)),
            pl.BlockSpec((None, 1, d), lambda i: (i // tps, 0, 0)),
            vec, _full(mu.shape),
            _full(wr.shape), _full(wk.shape), _full(wv.shape),
            _full(w1.shape), _full(w2.shape), _full(a1.shape), _full(a2.shape),
            _full(g1.shape), _full(g2.shape),
            vec, vec, vec, vec,
        ],
        out_specs=[row_spec] * 7 + [pl.BlockSpec((None, 1, d), lambda i: (i // tps, 0, 0))],
        out_shape=[jax.ShapeDtypeStruct((n, d), F32 if j == 1 else BF16) for j in range(7)]
        + [jax.ShapeDtypeStruct((b, 1, d), F32)],
        compiler_params=_cparams(("arbitrary",)),
        name="rwkv_pre",
    )(x, x, shift0, ln, mu, wr, wk, wv, w1, w2, a1, a2, g1, g2, w0, a0, k_k, k_a)
    return outs


def _scan_kernel(n_chunks, r_ref, lw_ref, k_ref, v_ref, a_ref, b_ref, s0_ref, y_ref, so_ref, s_scr):
    c = pl.program_id(1)
    C = CHUNK
    G = LANE_GROUP
    ri = lax.broadcasted_iota(jnp.int32, (G, G), 0)
    ci = lax.broadcasted_iota(jnp.int32, (G, G), 1)
    bd = (ri // HEAD) == (ci // HEAD)

    def expand(x):
        return jnp.where(bd, jnp.concatenate([x] * HEADS_PER_GROUP, axis=0), 0.0)

    n_seq, n_grp = s0_ref.shape[0], s0_ref.shape[1]
    chains = [(s, g) for s in range(n_seq) for g in range(n_grp)]
    n_groups = len(chains)

    def load_state():
        for i, (s, g) in enumerate(chains):
            s_scr[i] = expand(s0_ref[s, g])

    if n_chunks == 1:
        load_state()
    else:
        pl.when(c == 0)(load_state)

    tpos = lax.broadcasted_iota(jnp.int32, (C, G), 0)
    spos = lax.broadcasted_iota(jnp.int32, (C, G), 1) % C
    strict = spos < tpos
    incl = spos <= tpos
    tri = jnp.where(lax.broadcasted_iota(jnp.int32, (C, C), 0) >= lax.broadcasted_iota(jnp.int32, (C, C), 1),
                    1.0, 0.0).astype(BF16)
    steps = C.bit_length() - 1

    groups = range(n_groups)
    sls = [(s, slice(None), slice(g * G, (g + 1) * G)) for s, g in chains]
    lw = [lw_ref[sl] for sl in sls]
    v = [v_ref[sl] for sl in sls]
    cs = [_mm(_dot, tri, lw[g], 1, 2) for g in groups]
    p_in = [jnp.exp(cs[g]) for g in groups]
    p_inv = [jnp.exp(-cs[g]) for g in groups]
    p_end = [p_in[g][C - 1:C, :] for g in groups]
    rt = [r_ref[sls[g]] * p_in[g] for g in groups]
    at = [a_ref[sls[g]] * jnp.exp(cs[g] - lw[g]) for g in groups]
    bt = [b_ref[sls[g]] * p_inv[g] for g in groups]
    kt = [k_ref[sls[g]] * p_inv[g] for g in groups]
    S = [s_scr[g] for g in groups]
    la = [jnp.concatenate([at[g], rt[g]], axis=0) for g in groups]
    ab = [_mm(_dot_nt, la[g], expand(bt[g])) for g in groups]
    ak = [_mm(_dot_nt, la[g], expand(kt[g])) for g in groups]
    npow = [jnp.where(strict, ab[g][:C], 0.0) for g in groups]
    a_rb = [jnp.where(incl, ab[g][C:], 0.0) for g in groups]
    a_k = [jnp.where(jnp.concatenate([strict, incl], axis=0), ak[g], 0.0) for g in groups]
    from_v = [_mm(_dot, a_k[g], expand(v[g])) for g in groups]
    from_s = [_mm(_dot_nt, la[g], S[g]) for g in groups]
    w0 = [from_s[g][:C] + from_v[g][:C] for g in groups]
    y0 = [from_s[g][C:] + from_v[g][C:] for g in groups]
    eye = jnp.where(spos == tpos, 1.0, 0.0)
    tinv = [eye + npow[g] for g in groups]
    npow = [_mm(_dot, npow[g], expand(npow[g])) for g in groups]
    for i in range(1, steps):
        last = i + 1 == steps
        lhs = [tinv[g] if last else jnp.concatenate([tinv[g], npow[g]], axis=0) for g in groups]
        prod = [_mm(_dot, lhs[g], expand(npow[g])) for g in groups]
        tinv = [tinv[g] + prod[g][:C] for g in groups]
        if not last:
            npow = [prod[g][C:] for g in groups]
    u = [_mm(_dot, tinv[g], expand(w0[g])) for g in groups]
    for g in groups:
        y_ref[sls[g]] = y0[g] + _mm(_dot, a_rb[g], expand(u[g]))
    for g in groups:
        upd = _mm(_dot_tn, jnp.concatenate([u[g], v[g]], axis=0),
                  jnp.concatenate([bt[g], kt[g]], axis=0) * p_end[g])
        s_scr[g] = S[g] * p_end[g] + jnp.where(bd, upd, 0.0)

    def write_state():
        for i, (s, g) in enumerate(chains):
            acc = s_scr[i, 0:HEAD, :]
            for h in range(1, HEADS_PER_GROUP):
                acc = acc + s_scr[i, h * HEAD:(h + 1) * HEAD, :]
            so_ref[s, g] = acc

    if n_chunks == 1:
        write_state()
    else:
        pl.when(c == n_chunks - 1)(write_state)


def _wkv_scan(r, lw, k, v, av, bv, s0c, seq):
    n, d = r.shape
    b = n // seq
    ng = d // LANE_GROUP
    nc = seq // CHUNK
    sps = 4 if b % 4 == 0 else (2 if b % 2 == 0 else 1)
    blk = pl.BlockSpec((sps, CHUNK, d), lambda bi, ci: (bi, ci, 0))
    st = pl.BlockSpec((sps, ng, HEAD, LANE_GROUP), lambda bi, ci: (bi, 0, 0, 0))
    y, s_out = pl.pallas_call(
        functools.partial(_scan_kernel, nc),
        grid=(b // sps, nc),
        in_specs=[blk] * 6 + [st],
        out_specs=[blk, st],
        out_shape=[jax.ShapeDtypeStruct((b, seq, d), F32), jax.ShapeDtypeStruct(s0c.shape, F32)],
        scratch_shapes=[pltpu.VMEM((sps * ng, LANE_GROUP, LANE_GROUP), F32)],
        compiler_params=_cparams(("arbitrary", "arbitrary")),
        name="wkv_scan",
    )(*(z.reshape(b, seq, d) for z in (r, lw, k, v, av, bv)), s0c)
    return y.reshape(n, d), s_out


def _rwkv_post_kernel(y_ref, r_ref, k_ref, v_ref, g_ref, h_ref, lg_ref, lb_ref, rk_ref, wo_ref, o_ref):
    ones_bd = _group_ones()
    tm = y_ref.shape[0]
    sub = min(tm, SUB_ROWS)
    subs = [slice(s * sub, (s + 1) * sub) for s in range(tm // sub)]
    y = [y_ref[rows, :] for rows in subs]
    mean = [_head_sum(ys, ones_bd) * (1.0 / HEAD) for ys in y]
    rk = [_head_sum(r_ref[rows, :].astype(F32) * k_ref[rows, :].astype(F32) * rk_ref[...], ones_bd) for rows in subs]
    yc = [y[s] - mean[s] for s in range(len(subs))]
    var = [_head_sum(c * c, ones_bd) * (1.0 / HEAD) for c in yc]
    for s, rows in enumerate(subs):
        yn = yc[s] * lax.rsqrt(var[s] + GN_EPS) * lg_ref[...] + lb_ref[...]
        yn = yn + rk[s] * v_ref[rows, :].astype(F32)
        gated = (yn * g_ref[rows, :].astype(F32)).astype(BF16)
        o_ref[rows, :] = h_ref[rows, :] + _dot(gated, wo_ref[...])


def _rwkv_post(y, r, k, v, g, h, lnx_g, lnx_b, r_k, wo):
    n, d = y.shape
    tm = _row_tile(n, n, ROWS_MIX)
    row = pl.BlockSpec((tm, d), lambda i: (i, 0))
    vec = _full((1, d))
    return pl.pallas_call(
        _rwkv_post_kernel,
        grid=(n // tm,),
        in_specs=[row] * 6 + [vec, vec, vec, _full(wo.shape)],
        out_specs=row,
        out_shape=jax.ShapeDtypeStruct((n, d), F32),
        compiler_params=_cparams(("arbitrary",)),
        name="rwkv_post",
    )(y, r, k, v, g, h, lnx_g, lnx_b, r_k, wo)


def _ffn_kernel(final, proj, tf, h_ref, ln_ref, wg_ref, wu_ref, wd_ref, *rest):
    rest = list(rest)
    a_ref, wa_ref = (rest.pop(0), rest.pop(0)) if proj else (None, None)
    fg_ref = rest.pop(0) if final else None
    o_ref = rest.pop(0)
    y_ref = rest.pop(0) if final else None
    act_scr, = rest
    h = h_ref[...]
    if proj:
        h = h + _dot(a_ref[...], wa_ref[...])
    xn = _rms(h, ln_ref[...]).astype(BF16)
    for c in range(act_scr.shape[1] // tf):
        cols = slice(c * tf, (c + 1) * tf)
        gate = _dot(xn, wg_ref[:, cols])
        up = _dot(xn, wu_ref[:, cols])
        act_scr[:, cols] = (gate * _sigmoid(gate) * up).astype(BF16)
    out = h + _dot(act_scr[...], wd_ref[...])
    o_ref[...] = out
    if final:
        y_ref[...] = _rms(out, fg_ref[...])


def _ffn(h, ln, wg, wu, wd, final_g=None, proj=None):
    n, d = h.shape
    dff = wg.shape[1]
    tm = _row_tile(n, n, ROWS_MIX if proj is not None else ROWS_WIDE)
    tf = LANE_GROUP
    assert dff % tf == 0
    final = final_g is not None
    row = pl.BlockSpec((tm, d), lambda i: (i, 0))
    vec = pl.BlockSpec((1, d), lambda i: (0, 0))
    once = pl.Buffered(1)
    in_specs = [row, vec,
                pl.BlockSpec((d, dff), lambda i: (0, 0), pipeline_mode=once),
                pl.BlockSpec((d, dff), lambda i: (0, 0), pipeline_mode=once),
                pl.BlockSpec((dff, d), lambda i: (0, 0), pipeline_mode=once)]
    args = [h, ln, wg, wu, wd]
    out_specs = [row]
    out_shape = [jax.ShapeDtypeStruct((n, d), F32)]
    if proj is not None:
        in_specs += [row, pl.BlockSpec((d, d), lambda i: (0, 0), pipeline_mode=once)]
        args += list(proj)
    if final:
        in_specs.append(vec)
        args.append(final_g)
        out_specs.append(row)
        out_shape.append(jax.ShapeDtypeStruct((n, d), F32))
    res = pl.pallas_call(
        functools.partial(_ffn_kernel, final, proj is not None, tf),
        grid=(n // tm,),
        in_specs=in_specs,
        out_specs=out_specs,
        out_shape=out_shape,
        scratch_shapes=[pltpu.VMEM((tm, dff), BF16)],
        compiler_params=_cparams(("arbitrary",)),
        name="ffn_final" if final else "ffn",
    )(*args)
    return res


def _kvq_kernel(scale, h_ref, kvg_ref, ln_ref, wkt_ref, wvt_ref, wft_ref, bf_ref, wqt_ref,
                kt_out, vt_out, f_out, qt_out):
    h = h_ref[...]
    ms = jnp.mean(h * h, axis=-1, keepdims=True)
    hn = h * lax.rsqrt(ms + RMS_EPS)
    xkv = (hn * kvg_ref[...]).astype(BF16)
    xq = (hn * ln_ref[...]).astype(BF16)
    kt_out[...] = _dot_nt(wkt_ref[...], xkv)
    vt_out[...] = _dot_nt(wvt_ref[...], xkv)
    z = _dot_nt(wft_ref[...], xkv) + bf_ref[...]
    f_out[...] = -_softplus(-z)
    qt_out[...] = (_dot_nt(wqt_ref[...], xq) * scale).astype(BF16)


def _kvq(h, seq, kv_g, ln, wkt, wvt, wft, b_f, wqt, scale):
    n, d = h.shape
    b = n // seq
    nh = wft.shape[0]
    tm = _row_tile(n, seq, ROWS_WIDE)
    tps = seq // tm
    row = pl.BlockSpec((tm, d), lambda i: (i, 0))
    vec = _full((1, d))
    col = pl.BlockSpec((None, d, tm), lambda i: (i // tps, 0, i % tps))
    return pl.pallas_call(
        functools.partial(_kvq_kernel, scale),
        grid=(n // tm,),
        in_specs=[row, vec, vec, _full(wkt.shape), _full(wvt.shape), _full(wft.shape), _full((nh, 1)), _full(wqt.shape)],
        out_specs=[col, col, pl.BlockSpec((None, nh, tm), lambda i: (i // tps, 0, i % tps)), col],
        out_shape=[jax.ShapeDtypeStruct((b, d, seq), F32), jax.ShapeDtypeStruct((b, d, seq), F32),
                   jax.ShapeDtypeStruct((b, nh, seq), F32), jax.ShapeDtypeStruct((b, d, seq), BF16)],
        compiler_params=_cparams(("arbitrary",)),
        name="kvq_proj",
    )(h, kv_g, ln, wkt, wvt, wft, b_f, wqt)


def _cumsum_kernel(blk, f_ref, c0_ref, row_ref, *rest):
    aug_ref, last_ref = rest if len(rest) == 2 else (None, rest[0])
    nh, t = f_ref.shape
    triu = jnp.where(lax.broadcasted_iota(jnp.int32, (blk, blk), 0) <= lax.broadcasted_iota(jnp.int32, (blk, blk), 1),
                     1.0, 0.0).astype(BF16)
    head = lax.broadcasted_iota(jnp.int32, (nh, AUG_LANES), 0)
    lane = lax.broadcasted_iota(jnp.int32, (nh, AUG_LANES), 1)
    place = jnp.concatenate([jnp.where(lane == AUG_PIECES * head + i, 1.0, 0.0).astype(BF16)
                             for i in range(AUG_PIECES)], axis=0)
    carry = c0_ref[...]
    for j in range(t // blk):
        cols = slice(j * blk, (j + 1) * blk)
        part = _dot(jnp.concatenate(_split(f_ref[:, cols], 3), axis=0), triu)
        cb = part[0:nh] + part[nh:2 * nh] + part[2 * nh:3 * nh] + carry
        row_ref[:, cols] = cb
        if aug_ref is not None:
            pieces = jnp.concatenate(_split(cb * LOG2E, AUG_PIECES), axis=0)
            aug_ref[cols, :] = _dot_tn(pieces, place).astype(BF16)
        carry = cb[:, blk - 1:blk]
    last_ref[...] = carry


def _cumsum(logf, c0, with_aug):
    b, nh, t = logf.shape
    blk = min(t, LANE_GROUP)
    assert t % blk == 0 and AUG_PIECES * nh <= AUG_LANES
    row = pl.BlockSpec((None, nh, t), lambda i: (i, 0, 0))
    one = pl.BlockSpec((None, nh, 1), lambda i: (i, 0, 0))
    out_specs = [row, one]
    out_shape = [jax.ShapeDtypeStruct((b, nh, t), F32), jax.ShapeDtypeStruct((b, nh, 1), F32)]
    if with_aug:
        out_specs.insert(1, pl.BlockSpec((None, t, AUG_LANES), lambda i: (i, 0, 0)))
        out_shape.insert(1, jax.ShapeDtypeStruct((b, t, AUG_LANES), BF16))
    return pl.pallas_call(
        functools.partial(_cumsum_kernel, blk),
        grid=(b,),
        in_specs=[row, one],
        out_specs=out_specs,
        out_shape=out_shape,
        compiler_params=_cparams(("arbitrary",)),
        name="logf_cumsum",
    )(logf, c0)


def _head_pair_masks(shape):
    lane = lax.broadcasted_iota(jnp.int32, shape, len(shape) - 1)
    return lane < HEAD


def _attn_prompt_kernel(tk, q_ref, k_ref, v_ref, aug_ref, o_ref, ka_scr, vt_scr, st_scr, pm_scr, pt_scr):
    pi = pl.program_id(1)
    seq = q_ref.shape[1]
    nk = seq // tk
    vrows = HEAD + ONES_ROWS

    for kb in range(nk):
        ka_scr[kb * tk:(kb + 1) * tk, 0:2 * HEAD] = k_ref[:, kb * tk:(kb + 1) * tk].T.astype(BF16)
    ka_scr[:, 2 * HEAD:] = aug_ref[...]
    for h in range(2):
        vt_scr[h, 0:HEAD, :] = v_ref[h * HEAD:(h + 1) * HEAD, :].astype(BF16)
        vt_scr[h, HEAD:vrows, :] = jnp.ones((ONES_ROWS, seq), BF16)

    row = lax.broadcasted_iota(jnp.int32, (AUG_LANES, tk), 0)
    first_rows = lax.broadcasted_iota(jnp.int32, (2 * HEAD, tk), 0) < HEAD
    minus = []
    for h in range(2):
        lo = AUG_PIECES * (2 * pi + h)
        minus.append(jnp.where((row >= lo) & (row < lo + AUG_PIECES), -1.0, 0.0).astype(BF16))
    kpos = lax.broadcasted_iota(jnp.int32, (tk, tk), 0)
    qpos = lax.broadcasted_iota(jnp.int32, (tk, tk), 1)
    causal = kpos <= qpos

    def score_tiles(j):
        slot = j % 2
        q = q_ref[:, j * tk:(j + 1) * tk]
        zero = jnp.zeros_like(q)
        for h in range(2):
            qh = jnp.where(first_rows, q, zero) if h == 0 else jnp.where(first_rows, zero, q)
            qa = jnp.concatenate([qh, minus[h]], axis=0)
            pm = None
            for kb in range(j + 1):
                s = _dot(ka_scr[kb * tk:(kb + 1) * tk, :], qa)
                if kb == j:
                    s = jnp.where(causal, s, -jnp.inf)
                st_scr[slot, h, kb * tk:(kb + 1) * tk, :] = s
                part = jnp.max(s.reshape(tk // 8, 8, tk), axis=0)
                pm = part if pm is None else jnp.maximum(pm, part)
                yield
            pm_scr[slot, h] = pm

    def prob_tiles(j):
        slot = j % 2
        for h in range(2):
            mx = jnp.max(pm_scr[slot, h], axis=0, keepdims=True)
            for kb in range(j + 1):
                rows = slice(kb * tk, (kb + 1) * tk)
                pt_scr[slot, h, rows, :] = jnp.exp2(st_scr[slot, h, rows, :] - mx).astype(BF16)
                yield

    def finish(j):
        slot = j % 2
        outs = []
        for h in range(2):
            acc = _dot(vt_scr[h, :, 0:(j + 1) * tk], pt_scr[slot, h, 0:(j + 1) * tk, :])
            outs.append(acc[0:HEAD] / acc[HEAD:HEAD + 1])
        o_ref[j * tk:(j + 1) * tk, :] = jnp.concatenate(outs, axis=0).T.astype(o_ref.dtype)

    def run(*gens):
        gens = list(gens)
        while gens:
            for gen in list(gens):
                if next(gen, StopIteration) is StopIteration:
                    gens.remove(gen)

    run(score_tiles(0))
    run(*([score_tiles(1)] if nk > 1 else []), prob_tiles(0))
    for j in range(nk):
        stage = []
        if j + 2 < nk:
            stage.append(score_tiles(j + 2))
        if j + 1 < nk:
            stage.append(prob_tiles(j + 1))
        run(*stage)
        finish(j)


def _attn_prompt(qt, kt, vt, c_aug):
    b, d, seq = qt.shape
    npair = d // (2 * HEAD)
    tk = min(seq, LANE_GROUP)
    assert seq % tk == 0
    blk = pl.BlockSpec((seq, 2 * HEAD), lambda bi, pi: (bi, pi))
    blk_t = pl.BlockSpec((None, 2 * HEAD, seq), lambda bi, pi: (bi, pi, 0))
    return pl.pallas_call(
        functools.partial(_attn_prompt_kernel, tk),
        grid=(b, npair),
        in_specs=[blk_t, blk_t, blk_t, pl.BlockSpec((None, seq, AUG_LANES), lambda bi, pi: (bi, 0, 0))],
        out_specs=blk,
        out_shape=jax.ShapeDtypeStruct((b * seq, d), BF16),
        scratch_shapes=[pltpu.VMEM((seq, 2 * HEAD + AUG_LANES), BF16),
                        pltpu.VMEM((2, HEAD + ONES_ROWS, seq), BF16),
                        pltpu.VMEM((2, 2, seq, tk), F32),
                        pltpu.VMEM((2, 2, 8, tk), F32),
                        pltpu.VMEM((2, 2, seq, tk), BF16)],
        compiler_params=_cparams(("arbitrary", "arbitrary")),
        name="fox_attn_prompt",
    )(qt, kt, vt, c_aug)


def _attn_sample_kernel(q_ref, kp_ref, vp_ref, kn_ref, vn_ref, ckp_ref, ckn_ref, o_ref):
    pw = 2 * HEAD
    heads = [(p, h) for p in range(q_ref.shape[1] // pw) for h in range(2)]
    first = _head_pair_masks((q_ref.shape[0], pw))
    qpos = lax.broadcasted_iota(jnp.int32, (kn_ref.shape[1], kn_ref.shape[1]), 0)
    kpos = lax.broadcasted_iota(jnp.int32, (kn_ref.shape[1], kn_ref.shape[1]), 1)
    sp, sn = {}, {}
    for p, h in heads:
        q = q_ref[:, p * pw:(p + 1) * pw]
        qh = jnp.where(first, q, jnp.zeros_like(q)) if h == 0 else jnp.where(first, jnp.zeros_like(q), q)
        kp = kp_ref[p * pw:(p + 1) * pw, :].astype(BF16)
        kn = kn_ref[p * pw:(p + 1) * pw, :].astype(BF16)
        sp[p, h] = _dot(qh, kp) - ckp_ref[2 * p + h:2 * p + h + 1, :] * LOG2E
        sn[p, h] = jnp.where(kpos <= qpos, _dot(qh, kn) - ckn_ref[2 * p + h:2 * p + h + 1, :] * LOG2E, -jnp.inf)
    pp, pn, ls = {}, {}, {}
    for c in heads:
        m = jnp.maximum(jnp.max(sp[c], axis=-1, keepdims=True), jnp.max(sn[c], axis=-1, keepdims=True))
        ep = jnp.exp2(sp[c] - m)
        en = jnp.exp2(sn[c] - m)
        ls[c] = jnp.sum(ep, axis=-1, keepdims=True) + jnp.sum(en, axis=-1, keepdims=True)
        pp[c] = ep.astype(BF16)
        pn[c] = en.astype(BF16)
    for p in range(q_ref.shape[1] // pw):
        vp = vp_ref[p * pw:(p + 1) * pw, :].astype(BF16)
        vn = vn_ref[p * pw:(p + 1) * pw, :].astype(BF16)
        outs = [(_dot_nt(pp[p, h], vp) + _dot_nt(pn[p, h], vn)) / ls[p, h] for h in range(2)]
        o_ref[:, p * pw:(p + 1) * pw] = jnp.where(first, outs[0], outs[1]).astype(o_ref.dtype)


def _attn_sample(q, kpt, vpt, knt, vnt, ckp_row, ckn_row, seq, past):
    n, d = q.shape
    b = n // seq
    npair = d // (2 * HEAD)
    pps = 4 if npair % 4 == 0 else 1
    ng = npair // pps
    ckp = ckp_row.reshape(b, ng, 2 * pps, past)
    ckn = ckn_row.reshape(b, ng, 2 * pps, seq)
    rows = pl.BlockSpec((seq, pps * 2 * HEAD), lambda bi, pi: (bi, pi))
    new = pl.BlockSpec((None, pps * 2 * HEAD, seq), lambda bi, pi: (bi, pi, 0))
    old = pl.BlockSpec((None, pps * 2 * HEAD, past), lambda bi, pi: (bi, pi, 0))
    return pl.pallas_call(
        _attn_sample_kernel,
        grid=(b, ng),
        in_specs=[rows, old, old, new, new,
                  pl.BlockSpec((None, None, 2 * pps, past), lambda bi, pi: (bi, pi, 0, 0)),
                  pl.BlockSpec((None, None, 2 * pps, seq), lambda bi, pi: (bi, pi, 0, 0))],
        out_specs=rows,
        out_shape=jax.ShapeDtypeStruct((n, d), BF16),
        compiler_params=_cparams(("arbitrary", "arbitrary")),
        name="fox_attn_sample",
    )(q, kpt, vpt, knt, vnt, ckp, ckn)


def _pack_state(s):
    b, h, v, k = s.shape
    return s.reshape(b, h // HEADS_PER_GROUP, HEADS_PER_GROUP, v, k).transpose(0, 1, 3, 2, 4).reshape(
        b, h // HEADS_PER_GROUP, v, HEADS_PER_GROUP * k)


def _unpack_state(sc):
    b, g, v, gk = sc.shape
    k = gk // HEADS_PER_GROUP
    return sc.reshape(b, g, v, HEADS_PER_GROUP, k).transpose(0, 1, 3, 2, 4).reshape(b, g * HEADS_PER_GROUP, v, k)


def _trunk(x, shift0, wkv0, past, w):
    b, t, d = x.shape
    n = b * t
    h = x.reshape(n, d)
    n_a = w["w_r"].shape[0]
    new_shift, new_wkv = [], []
    for l in range(n_a):
        r, lw, k, v, av, bv, g, xl = _rwkv_pre(
            h, shift0[l].reshape(b, 1, d), t, w["ln1_g"][l], w["mu"][l], w["w_r"][l], w["w_k"][l], w["w_v"][l],
            w["w1"][l], w["w2"][l], w["a1"][l], w["a2"][l], w["g1"][l], w["g2"][l],
            w["w0"][l], w["a0"][l], w["k_k"][l], w["k_a"][l])
        tp = -(-t // CHUNK) * CHUNK
        scan_in = (r, lw, k, v, av, bv)
        if tp != t:
            scan_in = tuple(jnp.pad(z.reshape(b, t, d), ((0, 0), (0, tp - t), (0, 0))).reshape(b * tp, d)
                            for z in scan_in)
        y, s_out = _wkv_scan(*scan_in, _pack_state(wkv0[l]), tp)
        if tp != t:
            y = y.reshape(b, tp, d)[:, :t].reshape(n, d)
        h = _rwkv_post(y, r, k, v, g, h, w["lnx_g"][l], w["lnx_b"][l], w["r_k"][l], w["w_o"][l])
        h = _ffn(h, w["ln2_g"][l], w["w_gate"][l], w["w_up"][l], w["w_down"][l])[0]
        new_shift.append(xl.reshape(b, d))
        new_wkv.append(_unpack_state(s_out))

    nh = d // HEAD
    n_b = w["w_qt"].shape[0]
    scale = HEAD ** -0.5 * LOG2E
    kt_new, vt_new, logf, qt = _kvq(h, t, w["kv_g"], w["ln1_g"][n_a], w["w_kvf_kt"], w["w_kvf_vt"], w["w_kvf_ft"],
                                    w["b_f"], w["w_qt"][0], scale)
    zero_carry = jnp.zeros((b, nh, 1), F32)
    if past is None:
        _, c_aug, _ = _cumsum(logf, zero_carry, True)
    else:
        pk, pv, plf = past
        p = pk.shape[1]
        pkt = jnp.transpose(pk, (0, 2, 3, 1)).reshape(b, d, p).astype(F32)
        pvt = jnp.transpose(pv, (0, 2, 3, 1)).reshape(b, d, p).astype(F32)
        cp_row, cp_last = _cumsum(jnp.transpose(plf, (0, 2, 1)).astype(F32), zero_carry, False)
        c_row, _ = _cumsum(logf, cp_last, False)
    for j in range(n_b):
        l = n_a + j
        if j > 0:
            qt = _q_only(h, t, w["ln1_g"][l], w["w_qt"][j], scale)
        if past is None:
            o = _attn_prompt(qt, kt_new, vt_new, c_aug)
        else:
            q_rows = jnp.transpose(qt, (0, 2, 1)).reshape(n, d)
            o = _attn_sample(q_rows, pkt, pvt, kt_new, vt_new, cp_row, c_row, t, p)
        last = j == n_b - 1
        res = _ffn(h, w["ln2_g"][l], w["w_gate"][l], w["w_up"][l], w["w_down"][l],
                   w["final_g"] if last else None, proj=(o, w["w_ob"][j]))
        h = res[0]
        if last:
            y_out = res[1]
    k_new = jnp.transpose(kt_new.reshape(b, nh, HEAD, t), (0, 3, 1, 2))
    v_new = jnp.transpose(vt_new.reshape(b, nh, HEAD, t), (0, 3, 1, 2))
    return (y_out.reshape(b, t, d), jnp.stack(new_shift), jnp.stack(new_wkv),
            k_new, v_new, jnp.transpose(logf, (0, 2, 1)))


def _q_only_kernel(scale, h_ref, ln_ref, wqt_ref, qt_out):
    qt_out[...] = (_dot_nt(wqt_ref[...], _rms(h_ref[...], ln_ref[...]).astype(BF16)) * scale).astype(BF16)


def _q_only(h, seq, ln, wqt, scale):
    n, d = h.shape
    tm = _row_tile(n, seq, ROWS_MIX)
    tps = seq // tm
    return pl.pallas_call(
        functools.partial(_q_only_kernel, scale),
        grid=(n // tm,),
        in_specs=[pl.BlockSpec((tm, d), lambda i: (i, 0)), _full((1, d)), _full(wqt.shape)],
        out_specs=pl.BlockSpec((None, d, tm), lambda i: (i // tps, 0, i % tps)),
        out_shape=jax.ShapeDtypeStruct((n // seq, d, seq), BF16),
        compiler_params=_cparams(("arbitrary",)),
        name="q_proj",
    )(h, ln, wqt)


def kernel(x_prompt, x_sample, state_shift, state_wkv, cache_k, cache_v, cache_logf, ln1_g, ln2_g, w_gate, w_up, w_down, mu, w_r, w_k, w_v, w_o, w0, w1, w2, a0, a1, a2, g1, g2, k_k, k_a, r_k, lnx_g, lnx_b, kv_g, w_kvf, b_f, w_q, w_ob, final_g):
    d = x_prompt.shape[-1]
    nh = d // HEAD
    n_a = w_r.shape[0]
    bf = lambda z: z.astype(BF16)
    vec = lambda z: z.reshape(z.shape[:-1] + (1, d)).astype(F32)
    w = dict(
        ln1_g=vec(ln1_g), ln2_g=vec(ln2_g), w_gate=bf(w_gate), w_up=bf(w_up), w_down=bf(w_down),
        mu=mu.astype(F32), w_r=bf(w_r), w_k=bf(w_k), w_v=bf(w_v), w_o=bf(w_o),
        w0=vec(w0), w1=bf(w1), w2=bf(w2), a0=vec(a0), a1=bf(a1), a2=bf(a2), g1=bf(g1), g2=bf(g2),
        k_k=vec(k_k), k_a=vec(k_a),
        r_k=r_k.reshape(n_a, 1, d).astype(F32),
        lnx_g=vec(lnx_g), lnx_b=vec(lnx_b), kv_g=vec(kv_g),
        w_kvf_kt=bf(w_kvf[:, :d].T), w_kvf_vt=bf(w_kvf[:, d:2 * d].T), w_kvf_ft=bf(w_kvf[:, 2 * d:].T),
        b_f=b_f.reshape(nh, 1).astype(F32), w_qt=bf(jnp.swapaxes(w_q, 1, 2)), w_ob=bf(w_ob), final_g=vec(final_g),
    )
    bp = x_prompt.shape[0]
    dt = x_prompt.dtype
    y_p, shift_p, wkv_p, k_p, v_p, lf_p = _trunk(
        x_prompt, jnp.zeros((n_a, bp, d), dt), jnp.zeros((n_a, bp, nh, HEAD, HEAD), dt), None, w)
    y_s, shift_s, wkv_s, k_s, v_s, lf_s = _trunk(
        x_sample, state_shift, state_wkv, (cache_k, cache_v, cache_logf), w)
    return (y_p, y_s, shift_p, wkv_p, k_p, v_p, lf_p, shift_s, wkv_s, k_s, v_s, lf_s)
```

```python
import functools

import jax
import jax.numpy as jnp
from jax import lax
from jax.experimental import pallas as pl
from jax.experimental.pallas import tpu as pltpu

F32 = jnp.float32
BF16 = jnp.bfloat16

HEAD = 64
LANE_GROUP = 256
HEADS_PER_GROUP = LANE_GROUP // HEAD
CHUNK = 64
RMS_EPS = 1e-5
GN_EPS = 64e-5
LOG2E = 1.4426950408889634
NEG_EXP_M_HALF = -0.6065306597126334
AUG_LANES = 128
AUG_PIECES = 3
ONES_ROWS = 16
VMEM_LIMIT = 56 * 1024 * 1024
ROWS_MIX = 512
ROWS_WIDE = 1024
SUB_ROWS = 128


def _cparams(sem):
    return pltpu.CompilerParams(dimension_semantics=sem, vmem_limit_bytes=VMEM_LIMIT)


def _dot(a, b):
    return jnp.dot(a, b, preferred_element_type=F32)


def _dot_nt(a, b):
    return lax.dot_general(a, b, (((1,), (1,)), ((), ())), preferred_element_type=F32)


def _dot_tn(a, b):
    return lax.dot_general(a, b, (((0,), (0,)), ((), ())), preferred_element_type=F32)


def _split(x, parts):
    out = []
    rem = x
    for _ in range(parts):
        p = rem.astype(BF16)
        out.append(p)
        rem = rem - p.astype(F32)
    return out


def _mm(dot, a, b, pa=1, pb=1):
    a_parts = _split(a, pa)
    b_parts = _split(b, pb)
    acc = None
    for i, ap in enumerate(a_parts):
        for j, bp in enumerate(b_parts):
            if i + j >= max(pa, pb):
                continue
            t = dot(ap, bp)
            acc = t if acc is None else acc + t
    return acc


def _rms(x, g):
    ms = jnp.mean(x * x, axis=-1, keepdims=True)
    return x * lax.rsqrt(ms + RMS_EPS) * g


def _sigmoid(z):
    return 1.0 / (1.0 + jnp.exp2(z * (-LOG2E)))


def _softplus(z):
    return jnp.maximum(z, 0.0) + jnp.log(1.0 + jnp.exp(-jnp.abs(z)))


def _group_ones():
    r = lax.broadcasted_iota(jnp.int32, (LANE_GROUP, LANE_GROUP), 0) // HEAD
    c = lax.broadcasted_iota(jnp.int32, (LANE_GROUP, LANE_GROUP), 1) // HEAD
    return jnp.where(r == c, 1.0, 0.0).astype(BF16)


def _head_sum(x, ones_bd):
    outs = []
    for g in range(x.shape[1] // LANE_GROUP):
        outs.append(_dot(x[:, g * LANE_GROUP:(g + 1) * LANE_GROUP].astype(BF16), ones_bd))
    return jnp.concatenate(outs, axis=-1)


def _row_tile(n_rows, seq, cap):
    t = min(cap, seq)
    while seq % t:
        t //= 2
    assert t % 8 == 0 and n_rows % t == 0
    return t


def _full(shape):
    return pl.BlockSpec(shape, lambda *_: (0,) * len(shape))


def _rwkv_pre_kernel(tiles_per_seq, x_ref, xp_ref, sh_ref, ln_ref, mu_ref, wr_ref, wk_ref, wv_ref,
                     w1_ref, w2_ref, a1_ref, a2_ref, g1_ref, g2_ref, w0_ref, a0_ref, kk_ref, ka_ref,
                     r_out, lw_out, k_out, v_out, av_out, bv_out, g_out, xl_out):
    i = pl.program_id(0)
    ln = ln_ref[...]
    xn = _rms(x_ref[...], ln)
    tm = xn.shape[0]
    prev_tile_last = _rms(xp_ref[...], ln)[7:8, :]
    prev = jnp.where(i % tiles_per_seq == 0, sh_ref[...], prev_tile_last)
    row = lax.broadcasted_iota(jnp.int32, xn.shape, 0)
    x_prev = jnp.where(row == 0, prev, pltpu.roll(xn, 1, 0))
    xx = x_prev - xn
    xl_out[...] = xn[tm - 1:tm, :]

    sub = min(tm, SUB_ROWS)
    subs = [slice(s * sub, (s + 1) * sub) for s in range(tm // sub)]
    ones_bd = _group_ones()
    mixes = [[(xn[rows] + xx[rows] * mu_ref[j:j + 1, :]).astype(BF16) for j in range(6)] for rows in subs]
    r, k, v, lin = [], [], [], []
    for m in mixes:
        r.append(_dot(m[0], wr_ref[...]))
        k.append(_dot(m[2], wk_ref[...]))
        v.append(_dot(m[3], wv_ref[...]))
        lin.append((_dot(m[1], w1_ref[...]), _dot(m[4], a1_ref[...]), _dot(m[5], g1_ref[...])))
    act = [(jnp.tanh(lw).astype(BF16), la.astype(BF16), _sigmoid(lg).astype(BF16)) for lw, la, lg in lin]
    lout = [(_dot(aw, w2_ref[...]), _dot(aa, a2_ref[...]), _dot(ag, g2_ref[...])) for aw, aa, ag in act]
    kk = [ks * kk_ref[...] for ks in k]
    nsq = [_head_sum(kks * kks, ones_bd) for kks in kk]
    for s, rows in enumerate(subs):
        wl, al, g = lout[s]
        asig = _sigmoid(a0_ref[...] + al)
        kkn = kk[s] * lax.rsqrt(jnp.maximum(nsq[s], 1e-24))
        r_out[rows, :] = r[s].astype(r_out.dtype)
        lw_out[rows, :] = NEG_EXP_M_HALF * _sigmoid(w0_ref[...] + wl)
        k_out[rows, :] = (k[s] * (1.0 + (asig - 1.0) * ka_ref[...])).astype(k_out.dtype)
        v_out[rows, :] = v[s].astype(v_out.dtype)
        av_out[rows, :] = (-kkn).astype(av_out.dtype)
        bv_out[rows, :] = (kkn * asig).astype(bv_out.dtype)
        g_out[rows, :] = g.astype(g_out.dtype)


def _rwkv_pre(x, shift0, seq, ln, mu, wr, wk, wv, w1, w2, a1, a2, g1, g2, w0, a0, k_k, k_a):
    n, d = x.shape
    b = n // seq
    tm = _row_tile(n, seq, ROWS_MIX)
    tps = seq // tm
    row_spec = pl.BlockSpec((tm, d), lambda i: (i, 0))
    vec = _full((1, d))
    outs = pl.pallas_call(
        functools.partial(_rwkv_pre_kernel, tps),
        grid=(n // tm,),
        in_specs=[
            row_spec,
            pl.BlockSpec((8, d), lambda i: (jnp.maximum(i * (tm // 8) - 1, 0), 0)),
            pl.BlockSpec((None, 1, d), lambda i: (i // tps, 0, 0)),
            vec, _full(mu.shape),
            _full(wr.shape), _full(wk.shape), _full(wv.shape),
            _full(w1.shape), _full(w2.shape), _full(a1.shape), _full(a2.shape),
            _full(g1.shape), _full(g2.shape),
            vec, vec, vec, vec,
        ],
        out_specs=[row_spec] * 7 + [pl.BlockSpec((None, 1, d), lambda i: (i // tps, 0, 0))],
        out_shape=[jax.ShapeDtypeStruct((n, d), F32 if j == 1 else BF16) for j in range(7)]
        + [jax.ShapeDtypeStruct((b, 1, d), F32)],
        compiler_params=_cparams(("arbitrary",)),
        name="rwkv_pre",
    )(x, x, shift0, ln, mu, wr, wk, wv, w1, w2, a1, a2, g1, g2, w0, a0, k_k, k_a)
    return outs


def _scan_kernel(n_chunks, r_ref, lw_ref, k_ref, v_ref, a_ref, b_ref, s0_ref, y_ref, so_ref, s_scr):
    c = pl.program_id(1)
    C = CHUNK
    G = LANE_GROUP
    ri = lax.broadcasted_iota(jnp.int32, (G, G), 0)
    ci = lax.broadcasted_iota(jnp.int32, (G, G), 1)
    bd = (ri // HEAD) == (ci // HEAD)

    def expand(x):
        return jnp.where(bd, jnp.concatenate([x] * HEADS_PER_GROUP, axis=0), 0.0)

    n_seq, n_grp = s0_ref.shape[0], s0_ref.shape[1]
    chains = [(s, g) for s in range(n_seq) for g in range(n_grp)]
    n_groups = len(chains)

    def load_state():
        for i, (s, g) in enumerate(chains):
            s_scr[i] = expand(s0_ref[s, g])

    if n_chunks == 1:
        load_state()
    else:
        pl.when(c == 0)(load_state)

    tpos = lax.broadcasted_iota(jnp.int32, (C, G), 0)
    spos = lax.broadcasted_iota(jnp.int32, (C, G), 1) % C
    strict = spos < tpos
    incl = spos <= tpos
    tri = jnp.where(lax.broadcasted_iota(jnp.int32, (C, C), 0) >= lax.broadcasted_iota(jnp.int32, (C, C), 1),
                    1.0, 0.0).astype(BF16)
    steps = C.bit_length() - 1

    groups = range(n_groups)
    sls = [(s, slice(None), slice(g * G, (g + 1) * G)) for s, g in chains]
    lw = [lw_ref[sl] for sl in sls]
    v = [v_ref[sl] for sl in sls]
    cs = [_mm(_dot, tri, lw[g], 1, 2) for g in groups]
    p_in = [jnp.exp(cs[g]) for g in groups]
    p_inv = [jnp.exp(-cs[g]) for g in groups]
    p_end = [p_in[g][C - 1:C, :] for g in groups]
    rt = [r_ref[sls[g]] * p_in[g] for g in groups]
    at = [a_ref[sls[g]] * jnp.exp(cs[g] - lw[g]) for g in groups]
    bt = [b_ref[sls[g]] * p_inv[g] for g in groups]
    kt = [k_ref[sls[g]] * p_inv[g] for g in groups]
    S = [s_scr[g] for g in groups]
    la = [jnp.concatenate([at[g], rt[g]], axis=0) for g in groups]
    ab = [_mm(_dot_nt, la[g], expand(bt[g])) for g in groups]
    ak = [_mm(_dot_nt, la[g], expand(kt[g])) for g in groups]
    npow = [jnp.where(strict, ab[g][:C], 0.0) for g in groups]
    a_rb = [jnp.where(incl, ab[g][C:], 0.0) for g in groups]
    a_k = [jnp.where(jnp.concatenate([strict, incl], axis=0), ak[g], 0.0) for g in groups]
    from_v = [_mm(_dot, a_k[g], expand(v[g])) for g in groups]
    from_s = [_mm(_dot_nt, la[g], S[g]) for g in groups]
    w0 = [from_s[g][:C] + from_v[g][:C] for g in groups]
    y0 = [from_s[g][C:] + from_v[g][C:] for g in groups]
    eye = jnp.where(spos == tpos, 1.0, 0.0)
    tinv = [eye + npow[g] for g in groups]
    npow = [_mm(_dot, npow[g], expand(npow[g])) for g in groups]
    for i in range(1, steps):
        last = i + 1 == steps
        lhs = [tinv[g] if last else jnp.concatenate([tinv[g], npow[g]], axis=0) for g in groups]
        prod = [_mm(_dot, lhs[g], expand(npow[g])) for g in groups]
        tinv = [tinv[g] + prod[g][:C] for g in groups]
        if not last:
            npow = [prod[g][C:] for g in groups]
    u = [_mm(_dot, tinv[g], expand(w0[g])) for g in groups]
    for g in groups:
        y_ref[sls[g]] = y0[g] + _mm(_dot, a_rb[g], expand(u[g]))
    for g in groups:
        upd = _mm(_dot_tn, jnp.concatenate([u[g], v[g]], axis=0),
                  jnp.concatenate([bt[g], kt[g]], axis=0) * p_end[g])
        s_scr[g] = S[g] * p_end[g] + jnp.where(bd, upd, 0.0)

    def write_state():
        for i, (s, g) in enumerate(chains):
            acc = s_scr[i, 0:HEAD, :]
            for h in range(1, HEADS_PER_GROUP):
                acc = acc + s_scr[i, h * HEAD:(h + 1) * HEAD, :]
            so_ref[s, g] = acc

    if n_chunks == 1:
        write_state()
    else:
        pl.when(c == n_chunks - 1)(write_state)


def _wkv_scan(r, lw, k, v, av, bv, s0c, seq):
    n, d = r.shape
    b = n // seq
    ng = d // LANE_GROUP
    nc = seq // CHUNK
    sps = 4 if b % 4 == 0 else (2 if b % 2 == 0 else 1)
    blk = pl.BlockSpec((sps, CHUNK, d), lambda bi, ci: (bi, ci, 0))
    st = pl.BlockSpec((sps, ng, HEAD, LANE_GROUP), lambda bi, ci: (bi, 0, 0, 0))
    y, s_out = pl.pallas_call(
        functools.partial(_scan_kernel, nc),
        grid=(b // sps, nc),
        in_specs=[blk] * 6 + [st],
        out_specs=[blk, st],
        out_shape=[jax.ShapeDtypeStruct((b, seq, d), F32), jax.ShapeDtypeStruct(s0c.shape, F32)],
        scratch_shapes=[pltpu.VMEM((sps * ng, LANE_GROUP, LANE_GROUP), F32)],
        compiler_params=_cparams(("arbitrary", "arbitrary")),
        name="wkv_scan",
    )(*(z.reshape(b, seq, d) for z in (r, lw, k, v, av, bv)), s0c)
    return y.reshape(n, d), s_out


def _rwkv_post_kernel(y_ref, r_ref, k_ref, v_ref, g_ref, h_ref, lg_ref, lb_ref, rk_ref, wo_ref, o_ref):
    ones_bd = _group_ones()
    tm = y_ref.shape[0]
    sub = min(tm, SUB_ROWS)
    subs = [slice(s * sub, (s + 1) * sub) for s in range(tm // sub)]
    y = [y_ref[rows, :] for rows in subs]
    mean = [_head_sum(ys, ones_bd) * (1.0 / HEAD) for ys in y]
    rk = [_head_sum(r_ref[rows, :].astype(F32) * k_ref[rows, :].astype(F32) * rk_ref[...], ones_bd) for rows in subs]
    yc = [y[s] - mean[s] for s in range(len(subs))]
    var = [_head_sum(c * c, ones_bd) * (1.0 / HEAD) for c in yc]
    for s, rows in enumerate(subs):
        yn = yc[s] * lax.rsqrt(var[s] + GN_EPS) * lg_ref[...] + lb_ref[...]
        yn = yn + rk[s] * v_ref[rows, :].astype(F32)
        gated = (yn * g_ref[rows, :].astype(F32)).astype(BF16)
        o_ref[rows, :] = h_ref[rows, :] + _dot(gated, wo_ref[...])


def _rwkv_post(y, r, k, v, g, h, lnx_g, lnx_b, r_k, wo):
    n, d = y.shape
    tm = _row_tile(n, n, ROWS_MIX)
    row = pl.BlockSpec((tm, d), lambda i: (i, 0))
    vec = _full((1, d))
    return pl.pallas_call(
        _rwkv_post_kernel,
        grid=(n // tm,),
        in_specs=[row] * 6 + [vec, vec, vec, _full(wo.shape)],
        out_specs=row,
        out_shape=jax.ShapeDtypeStruct((n, d), F32),
        compiler_params=_cparams(("arbitrary",)),
        name="rwkv_post",
    )(y, r, k, v, g, h, lnx_g, lnx_b, r_k, wo)


def _ffn_kernel(final, proj, tf, h_ref, ln_ref, wg_ref, wu_ref, wd_ref, *rest):
    rest = list(rest)
    a_ref, wa_ref = (rest.pop(0), rest.pop(0)) if proj else (None, None)
    fg_ref = rest.pop(0) if final else None
    o_ref, act_scr = rest
    h = h_ref[...]
    if proj:
        h = h + _dot(a_ref[...], wa_ref[...])
    xn = _rms(h, ln_ref[...]).astype(BF16)
    for c in range(act_scr.shape[1] // tf):
        cols = slice(c * tf, (c + 1) * tf)
        gate = _dot(xn, wg_ref[:, cols])
        up = _dot(xn, wu_ref[:, cols])
        act_scr[:, cols] = (gate * _sigmoid(gate) * up).astype(BF16)
    out = h + _dot(act_scr[...], wd_ref[...])
    o_ref[...] = _rms(out, fg_ref[...]) if final else out


def _ffn(h, ln, wg, wu, wd, final_g=None, proj=None):
    n, d = h.shape
    dff = wg.shape[1]
    tm = _row_tile(n, n, ROWS_WIDE)
    tf = LANE_GROUP
    assert dff % tf == 0
    final = final_g is not None
    row = pl.BlockSpec((tm, d), lambda i: (i, 0))
    vec = pl.BlockSpec((1, d), lambda i: (0, 0))
    once = pl.Buffered(1)
    in_specs = [row, vec,
                pl.BlockSpec((d, dff), lambda i: (0, 0), pipeline_mode=once),
                pl.BlockSpec((d, dff), lambda i: (0, 0), pipeline_mode=once),
                pl.BlockSpec((dff, d), lambda i: (0, 0), pipeline_mode=once)]
    args = [h, ln, wg, wu, wd]
    if proj is not None:
        in_specs += [row, pl.BlockSpec((d, d), lambda i: (0, 0), pipeline_mode=once)]
        args += list(proj)
    if final:
        in_specs.append(vec)
        args.append(final_g)
    return pl.pallas_call(
        functools.partial(_ffn_kernel, final, proj is not None, tf),
        grid=(n // tm,),
        in_specs=in_specs,
        out_specs=row,
        out_shape=jax.ShapeDtypeStruct((n, d), F32),
        scratch_shapes=[pltpu.VMEM((tm, dff), BF16)],
        compiler_params=_cparams(("arbitrary",)),
        name="ffn_final" if final else "ffn",
    )(*args)


def _kvq_kernel(scale, h_ref, kvg_ref, ln_ref, wkt_ref, wvt_ref, wft_ref, bf_ref, wqt_ref,
                kt_out, vt_out, f_out, qt_out):
    h = h_ref[...]
    ms = jnp.mean(h * h, axis=-1, keepdims=True)
    hn = h * lax.rsqrt(ms + RMS_EPS)
    xkv = (hn * kvg_ref[...]).astype(BF16)
    xq = (hn * ln_ref[...]).astype(BF16)
    kt_out[...] = _dot_nt(wkt_ref[...], xkv)
    vt_out[...] = _dot_nt(wvt_ref[...], xkv)
    z = _dot_nt(wft_ref[...], xkv) + bf_ref[...]
    f_out[...] = -_softplus(-z)
    qt_out[...] = (_dot_nt(wqt_ref[...], xq) * scale).astype(BF16)


def _kvq(h, seq, kv_g, ln, wkt, wvt, wft, b_f, wqt, scale):
    n, d = h.shape
    b = n // seq
    nh = wft.shape[0]
    tm = _row_tile(n, seq, ROWS_WIDE)
    tps = seq // tm
    row = pl.BlockSpec((tm, d), lambda i: (i, 0))
    vec = _full((1, d))
    col = pl.BlockSpec((None, d, tm), lambda i: (i // tps, 0, i % tps))
    return pl.pallas_call(
        functools.partial(_kvq_kernel, scale),
        grid=(n // tm,),
        in_specs=[row, vec, vec, _full(wkt.shape), _full(wvt.shape), _full(wft.shape), _full((nh, 1)), _full(wqt.shape)],
        out_specs=[col, col, pl.BlockSpec((None, nh, tm), lambda i: (i // tps, 0, i % tps)), col],
        out_shape=[jax.ShapeDtypeStruct((b, d, seq), F32), jax.ShapeDtypeStruct((b, d, seq), F32),
                   jax.ShapeDtypeStruct((b, nh, seq), F32), jax.ShapeDtypeStruct((b, d, seq), BF16)],
        compiler_params=_cparams(("arbitrary",)),
        name="kvq_proj",
    )(h, kv_g, ln, wkt, wvt, wft, b_f, wqt)


def _cumsum_kernel(blk, f_ref, c0_ref, row_ref, *rest):
    aug_ref, last_ref = rest if len(rest) == 2 else (None, rest[0])
    nh, t = f_ref.shape
    triu = jnp.where(lax.broadcasted_iota(jnp.int32, (blk, blk), 0) <= lax.broadcasted_iota(jnp.int32, (blk, blk), 1),
                     1.0, 0.0).astype(BF16)
    head = lax.broadcasted_iota(jnp.int32, (nh, AUG_LANES), 0)
    lane = lax.broadcasted_iota(jnp.int32, (nh, AUG_LANES), 1)
    place = jnp.concatenate([jnp.where(lane == AUG_PIECES * head + i, 1.0, 0.0).astype(BF16)
                             for i in range(AUG_PIECES)], axis=0)
    carry = c0_ref[...]
    for j in range(t // blk):
        cols = slice(j * blk, (j + 1) * blk)
        part = _dot(jnp.concatenate(_split(f_ref[:, cols], 3), axis=0), triu)
        cb = part[0:nh] + part[nh:2 * nh] + part[2 * nh:3 * nh] + carry
        row_ref[:, cols] = cb
        if aug_ref is not None:
            pieces = jnp.concatenate(_split(cb * LOG2E, AUG_PIECES), axis=0)
            aug_ref[cols, :] = _dot_tn(pieces, place).astype(BF16)
        carry = cb[:, blk - 1:blk]
    last_ref[...] = carry


def _cumsum(logf, c0, with_aug):
    b, nh, t = logf.shape
    blk = min(t, LANE_GROUP)
    assert t % blk == 0 and AUG_PIECES * nh <= AUG_LANES
    row = pl.BlockSpec((None, nh, t), lambda i: (i, 0, 0))
    one = pl.BlockSpec((None, nh, 1), lambda i: (i, 0, 0))
    out_specs = [row, one]
    out_shape = [jax.ShapeDtypeStruct((b, nh, t), F32), jax.ShapeDtypeStruct((b, nh, 1), F32)]
    if with_aug:
        out_specs.insert(1, pl.BlockSpec((None, t, AUG_LANES), lambda i: (i, 0, 0)))
        out_shape.insert(1, jax.ShapeDtypeStruct((b, t, AUG_LANES), BF16))
    return pl.pallas_call(
        functools.partial(_cumsum_kernel, blk),
        grid=(b,),
        in_specs=[row, one],
        out_specs=out_specs,
        out_shape=out_shape,
        compiler_params=_cparams(("arbitrary",)),
        name="logf_cumsum",
    )(logf, c0)


def _head_pair_masks(shape):
    lane = lax.broadcasted_iota(jnp.int32, shape, len(shape) - 1)
    return lane < HEAD


def _attn_prompt_kernel(tk, q_ref, k_ref, v_ref, aug_ref, o_ref, ka_scr, vt_scr, st_scr, pm_scr, pt_scr):
    pi = pl.program_id(1)
    seq = q_ref.shape[1]
    nk = seq // tk
    vrows = HEAD + ONES_ROWS

    for kb in range(nk):
        ka_scr[kb * tk:(kb + 1) * tk, 0:2 * HEAD] = k_ref[:, kb * tk:(kb + 1) * tk].T.astype(BF16)
    ka_scr[:, 2 * HEAD:] = aug_ref[...]
    for h in range(2):
        vt_scr[h, 0:HEAD, :] = v_ref[h * HEAD:(h + 1) * HEAD, :].astype(BF16)
        vt_scr[h, HEAD:vrows, :] = jnp.ones((ONES_ROWS, seq), BF16)

    row = lax.broadcasted_iota(jnp.int32, (AUG_LANES, tk), 0)
    first_rows = lax.broadcasted_iota(jnp.int32, (2 * HEAD, tk), 0) < HEAD
    minus = []
    for h in range(2):
        lo = AUG_PIECES * (2 * pi + h)
        minus.append(jnp.where((row >= lo) & (row < lo + AUG_PIECES), -1.0, 0.0).astype(BF16))
    kpos = lax.broadcasted_iota(jnp.int32, (tk, tk), 0)
    qpos = lax.broadcasted_iota(jnp.int32, (tk, tk), 1)
    causal = kpos <= qpos

    def score_tiles(j):
        slot = j % 2
        q = q_ref[:, j * tk:(j + 1) * tk]
        zero = jnp.zeros_like(q)
        for h in range(2):
            qh = jnp.where(first_rows, q, zero) if h == 0 else jnp.where(first_rows, zero, q)
            qa = jnp.concatenate([qh, minus[h]], axis=0)
            pm = None
            for kb in range(j + 1):
                s = _dot(ka_scr[kb * tk:(kb + 1) * tk, :], qa)
                if kb == j:
                    s = jnp.where(causal, s, -jnp.inf)
                st_scr[slot, h, kb * tk:(kb + 1) * tk, :] = s
                part = jnp.max(s.reshape(tk // 8, 8, tk), axis=0)
                pm = part if pm is None else jnp.maximum(pm, part)
                yield
            pm_scr[slot, h] = pm

    def prob_tiles(j):
        slot = j % 2
        for h in range(2):
            mx = jnp.max(pm_scr[slot, h], axis=0, keepdims=True)
            for kb in range(j + 1):
                rows = slice(kb * tk, (kb + 1) * tk)
                pt_scr[slot, h, rows, :] = jnp.exp2(st_scr[slot, h, rows, :] - mx).astype(BF16)
                yield

    def finish(j):
        slot = j % 2
        outs = []
        for h in range(2):
            acc = _dot(vt_scr[h, :, 0:(j + 1) * tk], pt_scr[slot, h, 0:(j + 1) * tk, :])
            outs.append(acc[0:HEAD] / acc[HEAD:HEAD + 1])
        o_ref[j * tk:(j + 1) * tk, :] = jnp.concatenate(outs, axis=0).T.astype(o_ref.dtype)

    def run(*gens):
        gens = list(gens)
        while gens:
            for gen in list(gens):
                if next(gen, StopIteration) is StopIteration:
                    gens.remove(gen)

    run(score_tiles(0))
    run(*([score_tiles(1)] if nk > 1 else []), prob_tiles(0))
    for j in range(nk):
        stage = []
        if j + 2 < nk:
            stage.append(score_tiles(j + 2))
        if j + 1 < nk:
            stage.append(prob_tiles(j + 1))
        run(*stage)
        finish(j)


def _attn_prompt(qt, kt, vt, c_aug):
    b, d, seq = qt.shape
    npair = d // (2 * HEAD)
    tk = min(seq, LANE_GROUP)
    assert seq % tk == 0
    blk = pl.BlockSpec((seq, 2 * HEAD), lambda bi, pi: (bi, pi))
    blk_t = pl.BlockSpec((None, 2 * HEAD, seq), lambda bi, pi: (bi, pi, 0))
    return pl.pallas_call(
        functools.partial(_attn_prompt_kernel, tk),
        grid=(b, npair),
        in_specs=[blk_t, blk_t, blk_t, pl.BlockSpec((None, seq, AUG_LANES), lambda bi, pi: (bi, 0, 0))],
        out_specs=blk,
        out_shape=jax.ShapeDtypeStruct((b * seq, d), BF16),
        scratch_shapes=[pltpu.VMEM((seq, 2 * HEAD + AUG_LANES), BF16),
                        pltpu.VMEM((2, HEAD + ONES_ROWS, seq), BF16),
                        pltpu.VMEM((2, 2, seq, tk), F32),
                        pltpu.VMEM((2, 2, 8, tk), F32),
                        pltpu.VMEM((2, 2, seq, tk), BF16)],
        compiler_params=_cparams(("arbitrary", "arbitrary")),
        name="fox_attn_prompt",
    )(qt, kt, vt, c_aug)


def _attn_sample_kernel(q_ref, kp_ref, vp_ref, kn_ref, vn_ref, ckp_ref, ckn_ref, o_ref):
    pw = 2 * HEAD
    heads = [(p, h) for p in range(q_ref.shape[1] // pw) for h in range(2)]
    first = _head_pair_masks((q_ref.shape[0], pw))
    qpos = lax.broadcasted_iota(jnp.int32, (kn_ref.shape[1], kn_ref.shape[1]), 0)
    kpos = lax.broadcasted_iota(jnp.int32, (kn_ref.shape[1], kn_ref.shape[1]), 1)
    sp, sn = {}, {}
    for p, h in heads:
        q = q_ref[:, p * pw:(p + 1) * pw]
        qh = jnp.where(first, q, jnp.zeros_like(q)) if h == 0 else jnp.where(first, jnp.zeros_like(q), q)
        kp = kp_ref[p * pw:(p + 1) * pw, :].astype(BF16)
        kn = kn_ref[p * pw:(p + 1) * pw, :].astype(BF16)
        sp[p, h] = _dot(qh, kp) - ckp_ref[2 * p + h:2 * p + h + 1, :] * LOG2E
        sn[p, h] = jnp.where(kpos <= qpos, _dot(qh, kn) - ckn_ref[2 * p + h:2 * p + h + 1, :] * LOG2E, -jnp.inf)
    pp, pn, ls = {}, {}, {}
    for c in heads:
        m = jnp.maximum(jnp.max(sp[c], axis=-1, keepdims=True), jnp.max(sn[c], axis=-1, keepdims=True))
        ep = jnp.exp2(sp[c] - m)
        en = jnp.exp2(sn[c] - m)
        ls[c] = jnp.sum(ep, axis=-1, keepdims=True) + jnp.sum(en, axis=-1, keepdims=True)
        pp[c] = ep.astype(BF16)
        pn[c] = en.astype(BF16)
    for p in range(q_ref.shape[1] // pw):
        vp = vp_ref[p * pw:(p + 1) * pw, :].astype(BF16)
        vn = vn_ref[p * pw:(p + 1) * pw, :].astype(BF16)
        outs = [(_dot_nt(pp[p, h], vp) + _dot_nt(pn[p, h], vn)) / ls[p, h] for h in range(2)]
        o_ref[:, p * pw:(p + 1) * pw] = jnp.where(first, outs[0], outs[1]).astype(o_ref.dtype)


def _attn_sample(q, kpt, vpt, knt, vnt, ckp_row, ckn_row, seq, past):
    n, d = q.shape
    b = n // seq
    npair = d // (2 * HEAD)
    pps = 4 if npair % 4 == 0 else 1
    ng = npair // pps
    ckp = ckp_row.reshape(b, ng, 2 * pps, past)
    ckn = ckn_row.reshape(b, ng, 2 * pps, seq)
    rows = pl.BlockSpec((seq, pps * 2 * HEAD), lambda bi, pi: (bi, pi))
    new = pl.BlockSpec((None, pps * 2 * HEAD, seq), lambda bi, pi: (bi, pi, 0))
    old = pl.BlockSpec((None, pps * 2 * HEAD, past), lambda bi, pi: (bi, pi, 0))
    return pl.pallas_call(
        _attn_sample_kernel,
        grid=(b, ng),
        in_specs=[rows, old, old, new, new,
                  pl.BlockSpec((None, None, 2 * pps, past), lambda bi, pi: (bi, pi, 0, 0)),
                  pl.BlockSpec((None, None, 2 * pps, seq), lambda bi, pi: (bi, pi, 0, 0))],
        out_specs=rows,
        out_shape=jax.ShapeDtypeStruct((n, d), BF16),
        compiler_params=_cparams(("arbitrary", "arbitrary")),
        name="fox_attn_sample",
    )(q, kpt, vpt, knt, vnt, ckp, ckn)


def _pack_state(s):
    b, h, v, k = s.shape
    return s.reshape(b, h // HEADS_PER_GROUP, HEADS_PER_GROUP, v, k).transpose(0, 1, 3, 2, 4).reshape(
        b, h // HEADS_PER_GROUP, v, HEADS_PER_GROUP * k)


def _unpack_state(sc):
    b, g, v, gk = sc.shape
    k = gk // HEADS_PER_GROUP
    return sc.reshape(b, g, v, HEADS_PER_GROUP, k).transpose(0, 1, 3, 2, 4).reshape(b, g * HEADS_PER_GROUP, v, k)


def _trunk(x, shift0, wkv0, past, w):
    b, t, d = x.shape
    n = b * t
    h = x.reshape(n, d)
    n_a = w["w_r"].shape[0]
    new_shift, new_wkv = [], []
    for l in range(n_a):
        r, lw, k, v, av, bv, g, xl = _rwkv_pre(
            h, shift0[l].reshape(b, 1, d), t, w["ln1_g"][l], w["mu"][l], w["w_r"][l], w["w_k"][l], w["w_v"][l],
            w["w1"][l], w["w2"][l], w["a1"][l], w["a2"][l], w["g1"][l], w["g2"][l],
            w["w0"][l], w["a0"][l], w["k_k"][l], w["k_a"][l])
        tp = -(-t // CHUNK) * CHUNK
        scan_in = (r, lw, k, v, av, bv)
        if tp != t:
            scan_in = tuple(jnp.pad(z.reshape(b, t, d), ((0, 0), (0, tp - t), (0, 0))).reshape(b * tp, d)
                            for z in scan_in)
        y, s_out = _wkv_scan(*scan_in, _pack_state(wkv0[l]), tp)
        if tp != t:
            y = y.reshape(b, tp, d)[:, :t].reshape(n, d)
        h = _rwkv_post(y, r, k, v, g, h, w["lnx_g"][l], w["lnx_b"][l], w["r_k"][l], w["w_o"][l])
        h = _ffn(h, w["ln2_g"][l], w["w_gate"][l], w["w_up"][l], w["w_down"][l])
        new_shift.append(xl.reshape(b, d))
        new_wkv.append(_unpack_state(s_out))

    nh = d // HEAD
    n_b = w["w_qt"].shape[0]
    scale = HEAD ** -0.5 * LOG2E
    kt_new, vt_new, logf, qt = _kvq(h, t, w["kv_g"], w["ln1_g"][n_a], w["w_kvf_kt"], w["w_kvf_vt"], w["w_kvf_ft"],
                                    w["b_f"], w["w_qt"][0], scale)
    zero_carry = jnp.zeros((b, nh, 1), F32)
    if past is None:
        _, c_aug, _ = _cumsum(logf, zero_carry, True)
    else:
        pk, pv, plf = past
        p = pk.shape[1]
        pkt = jnp.transpose(pk, (0, 2, 3, 1)).reshape(b, d, p).astype(F32)
        pvt = jnp.transpose(pv, (0, 2, 3, 1)).reshape(b, d, p).astype(F32)
        cp_row, cp_last = _cumsum(jnp.transpose(plf, (0, 2, 1)).astype(F32), zero_carry, False)
        c_row, _ = _cumsum(logf, cp_last, False)
    for j in range(n_b):
        l = n_a + j
        if j > 0:
            qt = _q_only(h, t, w["ln1_g"][l], w["w_qt"][j], scale)
        if past is None:
            o = _attn_prompt(qt, kt_new, vt_new, c_aug)
        else:
            q_rows = jnp.transpose(qt, (0, 2, 1)).reshape(n, d)
            o = _attn_sample(q_rows, pkt, pvt, kt_new, vt_new, cp_row, c_row, t, p)
        last = j == n_b - 1
        h = _ffn(h, w["ln2_g"][l], w["w_gate"][l], w["w_up"][l], w["w_down"][l],
                 w["final_g"] if last else None, proj=(o, w["w_ob"][j]))
    y_out = h
    k_new = jnp.transpose(kt_new.reshape(b, nh, HEAD, t), (0, 3, 1, 2))
    v_new = jnp.transpose(vt_new.reshape(b, nh, HEAD, t), (0, 3, 1, 2))
    return (y_out.reshape(b, t, d), jnp.stack(new_shift), jnp.stack(new_wkv),
            k_new, v_new, jnp.transpose(logf, (0, 2, 1)))


def _q_only_kernel(scale, h_ref, ln_ref, wqt_ref, qt_out):
    qt_out[...] = (_dot_nt(wqt_ref[...], _rms(h_ref[...], ln_ref[...]).astype(BF16)) * scale).astype(BF16)


def _q_only(h, seq, ln, wqt, scale):
    n, d = h.shape
    tm = _row_tile(n, seq, ROWS_MIX)
    tps = seq // tm
    return pl.pallas_call(
        functools.partial(_q_only_kernel, scale),
        grid=(n // tm,),
        in_specs=[pl.BlockSpec((tm, d), lambda i: (i, 0)), _full((1, d)), _full(wqt.shape)],
        out_specs=pl.BlockSpec((None, d, tm), lambda i: (i // tps, 0, i % tps)),
        out_shape=jax.ShapeDtypeStruct((n // seq, d, seq), BF16),
        compiler_params=_cparams(("arbitrary",)),
        name="q_proj",
    )(h, ln, wqt)


def kernel(x_prompt, x_sample, state_shift, state_wkv, cache_k, cache_v, cache_logf, ln1_g, ln2_g, w_gate, w_up, w_down, mu, w_r, w_k, w_v, w_o, w0, w1, w2, a0, a1, a2, g1, g2, k_k, k_a, r_k, lnx_g, lnx_b, kv_g, w_kvf, b_f, w_q, w_ob, final_g):
    d = x_prompt.shape[-1]
    nh = d // HEAD
    n_a = w_r.shape[0]
    bf = lambda z: z.astype(BF16)
    vec = lambda z: z.reshape(z.shape[:-1] + (1, d)).astype(F32)
    w = dict(
        ln1_g=vec(ln1_g), ln2_g=vec(ln2_g), w_gate=bf(w_gate), w_up=bf(w_up), w_down=bf(w_down),
        mu=mu.astype(F32), w_r=bf(w_r), w_k=bf(w_k), w_v=bf(w_v), w_o=bf(w_o),
        w0=vec(w0), w1=bf(w1), w2=bf(w2), a0=vec(a0), a1=bf(a1), a2=bf(a2), g1=bf(g1), g2=bf(g2),
        k_k=vec(k_k), k_a=vec(k_a),
        r_k=r_k.reshape(n_a, 1, d).astype(F32),
        lnx_g=vec(lnx_g), lnx_b=vec(lnx_b), kv_g=vec(kv_g),
        w_kvf_kt=bf(w_kvf[:, :d].T), w_kvf_vt=bf(w_kvf[:, d:2 * d].T), w_kvf_ft=bf(w_kvf[:, 2 * d:].T),
        b_f=b_f.reshape(nh, 1).astype(F32), w_qt=bf(jnp.swapaxes(w_q, 1, 2)), w_ob=bf(w_ob), final_g=vec(final_g),
    )
    bp = x_prompt.shape[0]
    dt = x_prompt.dtype
    y_p, shift_p, wkv_p, k_p, v_p, lf_p = _trunk(
        x_prompt, jnp.zeros((n_a, bp, d), dt), jnp.zeros((n_a, bp, nh, HEAD, HEAD), dt), None, w)
    y_s, shift_s, wkv_s, k_s, v_s, lf_s = _trunk(
        x_sample, state_shift, state_wkv, (cache_k, cache_v, cache_logf), w)
    return (y_p, y_s, shift_p, wkv_p, k_p, v_p, lf_p, shift_s, wkv_s, k_s, v_s, lf_s)
```

```python
import functools

import jax
import jax.numpy as jnp
from jax import lax
from jax.experimental import pallas as pl
from jax.experimental.pallas import tpu as pltpu

F32 = jnp.float32
BF16 = jnp.bfloat16

HEAD = 64
LANE_GROUP = 256
HEADS_PER_GROUP = LANE_GROUP // HEAD
CHUNK = 64
RMS_EPS = 1e-5
GN_EPS = 64e-5
LOG2E = 1.4426950408889634
NEG_EXP_M_HALF = -0.6065306597126334
AUG_LANES = 128
AUG_PIECES = 3
ONES_ROWS = 16
VMEM_LIMIT = 56 * 1024 * 1024
ROWS_MIX = 512
ROWS_WIDE = 1024
SUB_ROWS = 128


def _cparams(sem):
    return pltpu.CompilerParams(dimension_semantics=sem, vmem_limit_bytes=VMEM_LIMIT)


def _dot(a, b):
    return jnp.dot(a, b, preferred_element_type=F32)


def _dot_nt(a, b):
    return lax.dot_general(a, b, (((1,), (1,)), ((), ())), preferred_element_type=F32)


def _dot_tn(a, b):
    return lax.dot_general(a, b, (((0,), (0,)), ((), ())), preferred_element_type=F32)


def _split(x, parts):
    out = []
    rem = x
    for _ in range(parts):
        p = rem.astype(BF16)
        out.append(p)
        rem = rem - p.astype(F32)
    return out


def _mm(dot, a, b, pa=1, pb=1):
    a_parts = _split(a, pa)
    b_parts = _split(b, pb)
    acc = None
    for i, ap in enumerate(a_parts):
        for j, bp in enumerate(b_parts):
            if i + j >= max(pa, pb):
                continue
            t = dot(ap, bp)
            acc = t if acc is None else acc + t
    return acc


def _rms(x, g):
    ms = jnp.mean(x * x, axis=-1, keepdims=True)
    return x * lax.rsqrt(ms + RMS_EPS) * g


def _sigmoid(z):
    return 1.0 / (1.0 + jnp.exp2(z * (-LOG2E)))


def _softplus(z):
    return jnp.maximum(z, 0.0) + jnp.log(1.0 + jnp.exp(-jnp.abs(z)))


def _group_ones():
    r = lax.broadcasted_iota(jnp.int32, (LANE_GROUP, LANE_GROUP), 0) // HEAD
    c = lax.broadcasted_iota(jnp.int32, (LANE_GROUP, LANE_GROUP), 1) // HEAD
    return jnp.where(r == c, 1.0, 0.0).astype(BF16)


def _head_sum(x, ones_bd):
    outs = []
    for g in range(x.shape[1] // LANE_GROUP):
        outs.append(_dot(x[:, g * LANE_GROUP:(g + 1) * LANE_GROUP].astype(BF16), ones_bd))
    return jnp.concatenate(outs, axis=-1)


def _row_tile(n_rows, seq, cap):
    t = min(cap, seq)
    while seq % t:
        t //= 2
    assert t % 8 == 0 and n_rows % t == 0
    return t


def _full(shape):
    return pl.BlockSpec(shape, lambda *_: (0,) * len(shape))


def _rwkv_pre_kernel(tiles_per_seq, x_ref, xp_ref, sh_ref, ln_ref, mu_ref, wr_ref, wk_ref, wv_ref,
                     w1_ref, w2_ref, a1_ref, a2_ref, g1_ref, g2_ref, w0_ref, a0_ref, kk_ref, ka_ref,
                     r_out, lw_out, k_out, v_out, av_out, bv_out, g_out, xl_out):
    i = pl.program_id(0)
    ln = ln_ref[...]
    xn = _rms(x_ref[...], ln)
    tm = xn.shape[0]
    prev_tile_last = _rms(xp_ref[...], ln)[7:8, :]
    prev = jnp.where(i % tiles_per_seq == 0, sh_ref[...], prev_tile_last)
    row = lax.broadcasted_iota(jnp.int32, xn.shape, 0)
    x_prev = jnp.where(row == 0, prev, pltpu.roll(xn, 1, 0))
    xx = x_prev - xn
    xl_out[...] = xn[tm - 1:tm, :]

    sub = min(tm, SUB_ROWS)
    subs = [slice(s * sub, (s + 1) * sub) for s in range(tm // sub)]
    ones_bd = _group_ones()
    mixes = [[(xn[rows] + xx[rows] * mu_ref[j:j + 1, :]).astype(BF16) for j in range(6)] for rows in subs]
    r, k, v, lin = [], [], [], []
    for m in mixes:
        r.append(_dot(m[0], wr_ref[...]))
        k.append(_dot(m[2], wk_ref[...]))
        v.append(_dot(m[3], wv_ref[...]))
        lin.append((_dot(m[1], w1_ref[...]), _dot(m[4], a1_ref[...]), _dot(m[5], g1_ref[...])))
    act = [(jnp.tanh(lw).astype(BF16), la.astype(BF16), _sigmoid(lg).astype(BF16)) for lw, la, lg in lin]
    lout = [(_dot(aw, w2_ref[...]), _dot(aa, a2_ref[...]), _dot(ag, g2_ref[...])) for aw, aa, ag in act]
    kk = [ks * kk_ref[...] for ks in k]
    nsq = [_head_sum(kks * kks, ones_bd) for kks in kk]
    for s, rows in enumerate(subs):
        wl, al, g = lout[s]
        asig = _sigmoid(a0_ref[...] + al)
        kkn = kk[s] * lax.rsqrt(jnp.maximum(nsq[s], 1e-24))
        r_out[rows, :] = r[s].astype(r_out.dtype)
        lw_out[rows, :] = NEG_EXP_M_HALF * _sigmoid(w0_ref[...] + wl)
        k_out[rows, :] = (k[s] * (1.0 + (asig - 1.0) * ka_ref[...])).astype(k_out.dtype)
        v_out[rows, :] = v[s].astype(v_out.dtype)
        av_out[rows, :] = (-kkn).astype(av_out.dtype)
        bv_out[rows, :] = (kkn * asig).astype(bv_out.dtype)
        g_out[rows, :] = g.astype(g_out.dtype)


def _rwkv_pre(x, shift0, seq, ln, mu, wr, wk, wv, w1, w2, a1, a2, g1, g2, w0, a0, k_k, k_a):
    n, d = x.shape
    b = n // seq
    tm = _row_tile(n, seq, ROWS_MIX)
    tps = seq // tm
    row_spec = pl.BlockSpec((tm, d), lambda i: (i, 0))
    vec = _full((1, d))
    outs = pl.pallas_call(
        functools.partial(_rwkv_pre_kernel, tps),
        grid=(n // tm,),
        in_specs=[
            row_spec,
            pl.BlockSpec((8, d), lambda i: (jnp.maximum(i * (tm // 8) - 1, 0), 0)),
            pl.BlockSpec((None, 1, d), lambda i: (i // tps, 0, 0)),
            vec, _full(mu.shape),
            _full(wr.shape), _full(wk.shape), _full(wv.shape),
            _full(w1.shape), _full(w2.shape), _full(a1.shape), _full(a2.shape),
            _full(g1.shape), _full(g2.shape),
            vec, vec, vec, vec,
        ],
        out_specs=[row_spec] * 7 + [pl.BlockSpec((None, 1, d), lambda i: (i // tps, 0, 0))],
        out_shape=[jax.ShapeDtypeStruct((n, d), F32 if j == 1 else BF16) for j in range(7)]
        + [jax.ShapeDtypeStruct((b, 1, d), F32)],
        compiler_params=_cparams(("arbitrary",)),
        name="rwkv_pre",
    )(x, x, shift0, ln, mu, wr, wk, wv, w1, w2, a1, a2, g1, g2, w0, a0, k_k, k_a)
    return outs


def _scan_kernel(n_chunks, r_ref, lw_ref, k_ref, v_ref, a_ref, b_ref, s0_ref, y_ref, so_ref, s_scr):
    c = pl.program_id(1)
    C = CHUNK
    G = LANE_GROUP
    ri = lax.broadcasted_iota(jnp.int32, (G, G), 0)
    ci = lax.broadcasted_iota(jnp.int32, (G, G), 1)
    bd = (ri // HEAD) == (ci // HEAD)

    def expand(x):
        return jnp.where(bd, jnp.concatenate([x] * HEADS_PER_GROUP, axis=0), 0.0)

    n_seq, n_grp = s0_ref.shape[0], s0_ref.shape[1]
    chains = [(s, g) for s in range(n_seq) for g in range(n_grp)]
    n_groups = len(chains)

    def load_state():
        for i, (s, g) in enumerate(chains):
            s_scr[i] = expand(s0_ref[s, g])

    if n_chunks == 1:
        load_state()
    else:
        pl.when(c == 0)(load_state)

    tpos = lax.broadcasted_iota(jnp.int32, (C, G), 0)
    spos = lax.broadcasted_iota(jnp.int32, (C, G), 1) % C
    strict = spos < tpos
    incl = spos <= tpos
    tri = jnp.where(lax.broadcasted_iota(jnp.int32, (C, C), 0) >= lax.broadcasted_iota(jnp.int32, (C, C), 1),
                    1.0, 0.0).astype(BF16)
    steps = C.bit_length() - 1

    groups = range(n_groups)
    sls = [(s, slice(None), slice(g * G, (g + 1) * G)) for s, g in chains]
    lw = [lw_ref[sl] for sl in sls]
    v = [v_ref[sl] for sl in sls]
    cs = [_mm(_dot, tri, lw[g], 1, 2) for g in groups]
    p_in = [jnp.exp(cs[g]) for g in groups]
    p_inv = [jnp.exp(-cs[g]) for g in groups]
    p_end = [p_in[g][C - 1:C, :] for g in groups]
    rt = [r_ref[sls[g]] * p_in[g] for g in groups]
    at = [a_ref[sls[g]] * jnp.exp(cs[g] - lw[g]) for g in groups]
    bt = [b_ref[sls[g]] * p_inv[g] for g in groups]
    kt = [k_ref[sls[g]] * p_inv[g] for g in groups]
    S = [s_scr[g] for g in groups]
    la = [jnp.concatenate([at[g], rt[g]], axis=0) for g in groups]
    ab = [_mm(_dot_nt, la[g], expand(bt[g])) for g in groups]
    ak = [_mm(_dot_nt, la[g], expand(kt[g])) for g in groups]
    npow = [jnp.where(strict, ab[g][:C], 0.0) for g in groups]
    a_rb = [jnp.where(incl, ab[g][C:], 0.0) for g in groups]
    a_k = [jnp.where(jnp.concatenate([strict, incl], axis=0), ak[g], 0.0) for g in groups]
    from_v = [_mm(_dot, a_k[g], expand(v[g])) for g in groups]
    from_s = [_mm(_dot_nt, la[g], S[g]) for g in groups]
    w0 = [from_s[g][:C] + from_v[g][:C] for g in groups]
    y0 = [from_s[g][C:] + from_v[g][C:] for g in groups]
    eye = jnp.where(spos == tpos, 1.0, 0.0)
    tinv = [eye + npow[g] for g in groups]
    npow = [_mm(_dot, npow[g], expand(npow[g])) for g in groups]
    for i in range(1, steps):
        last = i + 1 == steps
        lhs = [tinv[g] if last else jnp.concatenate([tinv[g], npow[g]], axis=0) for g in groups]
        prod = [_mm(_dot, lhs[g], expand(npow[g])) for g in groups]
        tinv = [tinv[g] + prod[g][:C] for g in groups]
        if not last:
            npow = [prod[g][C:] for g in groups]
    u = [_mm(_dot, tinv[g], expand(w0[g])) for g in groups]
    for g in groups:
        y_ref[sls[g]] = y0[g] + _mm(_dot, a_rb[g], expand(u[g]))
    for g in groups:
        upd = _mm(_dot_tn, jnp.concatenate([u[g], v[g]], axis=0),
                  jnp.concatenate([bt[g], kt[g]], axis=0) * p_end[g])
        s_scr[g] = S[g] * p_end[g] + jnp.where(bd, upd, 0.0)

    def write_state():
        for i, (s, g) in enumerate(chains):
            acc = s_scr[i, 0:HEAD, :]
            for h in range(1, HEADS_PER_GROUP):
                acc = acc + s_scr[i, h * HEAD:(h + 1) * HEAD, :]
            so_ref[s, g] = acc

    if n_chunks == 1:
        write_state()
    else:
        pl.when(c == n_chunks - 1)(write_state)


def _wkv_scan(r, lw, k, v, av, bv, s0c, seq):
    n, d = r.shape
    b = n // seq
    ng = d // LANE_GROUP
    nc = seq // CHUNK
    sps = 4 if b % 4 == 0 else (2 if b % 2 == 0 else 1)
    blk = pl.BlockSpec((sps, CHUNK, d), lambda bi, ci: (bi, ci, 0))
    st = pl.BlockSpec((sps, ng, HEAD, LANE_GROUP), lambda bi, ci: (bi, 0, 0, 0))
    y, s_out = pl.pallas_call(
        functools.partial(_scan_kernel, nc),
        grid=(b // sps, nc),
        in_specs=[blk] * 6 + [st],
        out_specs=[blk, st],
        out_shape=[jax.ShapeDtypeStruct((b, seq, d), F32), jax.ShapeDtypeStruct(s0c.shape, F32)],
        scratch_shapes=[pltpu.VMEM((sps * ng, LANE_GROUP, LANE_GROUP), F32)],
        compiler_params=_cparams(("arbitrary", "arbitrary")),
        name="wkv_scan",
    )(*(z.reshape(b, seq, d) for z in (r, lw, k, v, av, bv)), s0c)
    return y.reshape(n, d), s_out


def _rwkv_out(y_ref, r_ref, k_ref, v_ref, g_ref, lg_ref, lb_ref, rk_ref, wo_ref):
    ones_bd = _group_ones()
    tm = y_ref.shape[0]
    sub = min(tm, SUB_ROWS)
    subs = [slice(s * sub, (s + 1) * sub) for s in range(tm // sub)]
    y = [y_ref[rows, :] for rows in subs]
    mean = [_head_sum(ys, ones_bd) * (1.0 / HEAD) for ys in y]
    rk = [_head_sum(r_ref[rows, :].astype(F32) * k_ref[rows, :].astype(F32) * rk_ref[...], ones_bd) for rows in subs]
    yc = [y[s] - mean[s] for s in range(len(subs))]
    var = [_head_sum(c * c, ones_bd) * (1.0 / HEAD) for c in yc]
    outs = []
    for s, rows in enumerate(subs):
        yn = yc[s] * lax.rsqrt(var[s] + GN_EPS) * lg_ref[...] + lb_ref[...]
        yn = yn + rk[s] * v_ref[rows, :].astype(F32)
        gated = (yn * g_ref[rows, :].astype(F32)).astype(BF16)
        outs.append(_dot(gated, wo_ref[...]))
    return jnp.concatenate(outs, axis=0)


def _ffn_kernel(final, proj, mix, tf, h_ref, ln_ref, wg_ref, wu_ref, wd_ref, *rest):
    rest = list(rest)
    mix_refs = [rest.pop(0) for _ in range(9)] if mix else None
    a_ref, wa_ref = (rest.pop(0), rest.pop(0)) if proj else (None, None)
    fg_ref = rest.pop(0) if final else None
    o_ref, act_scr = rest
    h = h_ref[...]
    if mix:
        h = h + _rwkv_out(*mix_refs)
    if proj:
        h = h + _dot(a_ref[...], wa_ref[...])
    xn = _rms(h, ln_ref[...]).astype(BF16)
    for c in range(act_scr.shape[1] // tf):
        cols = slice(c * tf, (c + 1) * tf)
        gate = _dot(xn, wg_ref[:, cols])
        up = _dot(xn, wu_ref[:, cols])
        act_scr[:, cols] = (gate * _sigmoid(gate) * up).astype(BF16)
    out = h + _dot(act_scr[...], wd_ref[...])
    o_ref[...] = _rms(out, fg_ref[...]) if final else out


def _ffn(h, ln, wg, wu, wd, final_g=None, proj=None, mix=None):
    n, d = h.shape
    dff = wg.shape[1]
    tm = _row_tile(n, n, ROWS_MIX if mix is not None else ROWS_WIDE)
    tf = LANE_GROUP
    assert dff % tf == 0
    final = final_g is not None
    row = pl.BlockSpec((tm, d), lambda i: (i, 0))
    vec = pl.BlockSpec((1, d), lambda i: (0, 0))
    once = pl.Buffered(1)
    square = pl.BlockSpec((d, d), lambda i: (0, 0), pipeline_mode=once)
    in_specs = [row, vec,
                pl.BlockSpec((d, dff), lambda i: (0, 0), pipeline_mode=once),
                pl.BlockSpec((d, dff), lambda i: (0, 0), pipeline_mode=once),
                pl.BlockSpec((dff, d), lambda i: (0, 0), pipeline_mode=once)]
    args = [h, ln, wg, wu, wd]
    if mix is not None:
        in_specs += [row] * 5 + [vec] * 3 + [square]
        args += list(mix)
    if proj is not None:
        in_specs += [row, square]
        args += list(proj)
    if final:
        in_specs.append(vec)
        args.append(final_g)
    return pl.pallas_call(
        functools.partial(_ffn_kernel, final, proj is not None, mix is not None, tf),
        grid=(n // tm,),
        in_specs=in_specs,
        out_specs=row,
        out_shape=jax.ShapeDtypeStruct((n, d), F32),
        scratch_shapes=[pltpu.VMEM((tm, dff), BF16)],
        compiler_params=_cparams(("arbitrary",)),
        name="ffn_final" if final else "ffn",
    )(*args)


def _kvq_kernel(scale, h_ref, kvg_ref, ln_ref, wkt_ref, wvt_ref, wft_ref, bf_ref, wqt_ref,
                kt_out, vt_out, f_out, qt_out):
    h = h_ref[...]
    ms = jnp.mean(h * h, axis=-1, keepdims=True)
    hn = h * lax.rsqrt(ms + RMS_EPS)
    xkv = (hn * kvg_ref[...]).astype(BF16)
    xq = (hn * ln_ref[...]).astype(BF16)
    kt_out[...] = _dot_nt(wkt_ref[...], xkv)
    vt_out[...] = _dot_nt(wvt_ref[...], xkv)
    z = _dot_nt(wft_ref[...], xkv) + bf_ref[...]
    f_out[...] = -_softplus(-z)
    qt_out[...] = (_dot_nt(wqt_ref[...], xq) * scale).astype(BF16)


def _kvq(h, seq, kv_g, ln, wkt, wvt, wft, b_f, wqt, scale):
    n, d = h.shape
    b = n // seq
    nh = wft.shape[0]
    tm = _row_tile(n, seq, ROWS_WIDE)
    tps = seq // tm
    row = pl.BlockSpec((tm, d), lambda i: (i, 0))
    vec = _full((1, d))
    col = pl.BlockSpec((None, d, tm), lambda i: (i // tps, 0, i % tps))
    return pl.pallas_call(
        functools.partial(_kvq_kernel, scale),
        grid=(n // tm,),
        in_specs=[row, vec, vec, _full(wkt.shape), _full(wvt.shape), _full(wft.shape), _full((nh, 1)), _full(wqt.shape)],
        out_specs=[col, col, pl.BlockSpec((None, nh, tm), lambda i: (i // tps, 0, i % tps)), col],
        out_shape=[jax.ShapeDtypeStruct((b, d, seq), F32), jax.ShapeDtypeStruct((b, d, seq), F32),
                   jax.ShapeDtypeStruct((b, nh, seq), F32), jax.ShapeDtypeStruct((b, d, seq), BF16)],
        compiler_params=_cparams(("arbitrary",)),
        name="kvq_proj",
    )(h, kv_g, ln, wkt, wvt, wft, b_f, wqt)


def _cumsum_kernel(blk, f_ref, c0_ref, row_ref, *rest):
    aug_ref, last_ref = rest if len(rest) == 2 else (None, rest[0])
    nh, t = f_ref.shape
    triu = jnp.where(lax.broadcasted_iota(jnp.int32, (blk, blk), 0) <= lax.broadcasted_iota(jnp.int32, (blk, blk), 1),
                     1.0, 0.0).astype(BF16)
    head = lax.broadcasted_iota(jnp.int32, (nh, AUG_LANES), 0)
    lane = lax.broadcasted_iota(jnp.int32, (nh, AUG_LANES), 1)
    place = jnp.concatenate([jnp.where(lane == AUG_PIECES * head + i, 1.0, 0.0).astype(BF16)
                             for i in range(AUG_PIECES)], axis=0)
    carry = c0_ref[...]
    for j in range(t // blk):
        cols = slice(j * blk, (j + 1) * blk)
        part = _dot(jnp.concatenate(_split(f_ref[:, cols], 3), axis=0), triu)
        cb = part[0:nh] + part[nh:2 * nh] + part[2 * nh:3 * nh] + carry
        row_ref[:, cols] = cb
        if aug_ref is not None:
            pieces = jnp.concatenate(_split(cb * LOG2E, AUG_PIECES), axis=0)
            aug_ref[cols, :] = _dot_tn(pieces, place).astype(BF16)
        carry = cb[:, blk - 1:blk]
    last_ref[...] = carry


def _cumsum(logf, c0, with_aug):
    b, nh, t = logf.shape
    blk = min(t, LANE_GROUP)
    assert t % blk == 0 and AUG_PIECES * nh <= AUG_LANES
    row = pl.BlockSpec((None, nh, t), lambda i: (i, 0, 0))
    one = pl.BlockSpec((None, nh, 1), lambda i: (i, 0, 0))
    out_specs = [row, one]
    out_shape = [jax.ShapeDtypeStruct((b, nh, t), F32), jax.ShapeDtypeStruct((b, nh, 1), F32)]
    if with_aug:
        out_specs.insert(1, pl.BlockSpec((None, t, AUG_LANES), lambda i: (i, 0, 0)))
        out_shape.insert(1, jax.ShapeDtypeStruct((b, t, AUG_LANES), BF16))
    return pl.pallas_call(
        functools.partial(_cumsum_kernel, blk),
        grid=(b,),
        in_specs=[row, one],
        out_specs=out_specs,
        out_shape=out_shape,
        compiler_params=_cparams(("arbitrary",)),
        name="logf_cumsum",
    )(logf, c0)


def _head_pair_masks(shape):
    lane = lax.broadcasted_iota(jnp.int32, shape, len(shape) - 1)
    return lane < HEAD


def _attn_prompt_kernel(tk, q_ref, k_ref, v_ref, aug_ref, o_ref, ka_scr, vt_scr, st_scr, pm_scr, pt_scr):
    pi = pl.program_id(1)
    seq = q_ref.shape[1]
    nk = seq // tk
    vrows = HEAD + ONES_ROWS

    for kb in range(nk):
        ka_scr[kb * tk:(kb + 1) * tk, 0:2 * HEAD] = k_ref[:, kb * tk:(kb + 1) * tk].T.astype(BF16)
    ka_scr[:, 2 * HEAD:] = aug_ref[...]
    for h in range(2):
        vt_scr[h, 0:HEAD, :] = v_ref[h * HEAD:(h + 1) * HEAD, :].astype(BF16)
        vt_scr[h, HEAD:vrows, :] = jnp.ones((ONES_ROWS, seq), BF16)

    row = lax.broadcasted_iota(jnp.int32, (AUG_LANES, tk), 0)
    first_rows = lax.broadcasted_iota(jnp.int32, (2 * HEAD, tk), 0) < HEAD
    minus = []
    for h in range(2):
        lo = AUG_PIECES * (2 * pi + h)
        minus.append(jnp.where((row >= lo) & (row < lo + AUG_PIECES), -1.0, 0.0).astype(BF16))
    kpos = lax.broadcasted_iota(jnp.int32, (tk, tk), 0)
    qpos = lax.broadcasted_iota(jnp.int32, (tk, tk), 1)
    causal = kpos <= qpos

    def score_tiles(j):
        slot = j % 2
        q = q_ref[:, j * tk:(j + 1) * tk]
        zero = jnp.zeros_like(q)
        for h in range(2):
            qh = jnp.where(first_rows, q, zero) if h == 0 else jnp.where(first_rows, zero, q)
            qa = jnp.concatenate([qh, minus[h]], axis=0)
            pm = None
            for kb in range(j + 1):
                s = _dot(ka_scr[kb * tk:(kb + 1) * tk, :], qa)
                if kb == j:
                    s = jnp.where(causal, s, -jnp.inf)
                st_scr[slot, h, kb * tk:(kb + 1) * tk, :] = s
                part = jnp.max(s.reshape(tk // 8, 8, tk), axis=0)
                pm = part if pm is None else jnp.maximum(pm, part)
                yield
            pm_scr[slot, h] = pm

    def prob_tiles(j):
        slot = j % 2
        for h in range(2):
            mx = jnp.max(pm_scr[slot, h], axis=0, keepdims=True)
            for kb in range(j + 1):
                rows = slice(kb * tk, (kb + 1) * tk)
                pt_scr[slot, h, rows, :] = jnp.exp2(st_scr[slot, h, rows, :] - mx).astype(BF16)
                yield

    def finish(j):
        slot = j % 2
        outs = []
        for h in range(2):
            acc = _dot(vt_scr[h, :, 0:(j + 1) * tk], pt_scr[slot, h, 0:(j + 1) * tk, :])
            outs.append(acc[0:HEAD] / acc[HEAD:HEAD + 1])
        o_ref[j * tk:(j + 1) * tk, :] = jnp.concatenate(outs, axis=0).T.astype(o_ref.dtype)

    def run(*gens):
        gens = list(gens)
        while gens:
            for gen in list(gens):
                if next(gen, StopIteration) is StopIteration:
                    gens.remove(gen)

    run(score_tiles(0))
    run(*([score_tiles(1)] if nk > 1 else []), prob_tiles(0))
    for j in range(nk):
        stage = []
        if j + 2 < nk:
            stage.append(score_tiles(j + 2))
        if j + 1 < nk:
            stage.append(prob_tiles(j + 1))
        run(*stage)
        finish(j)


def _attn_prompt(qt, kt, vt, c_aug):
    b, d, seq = qt.shape
    npair = d // (2 * HEAD)
    tk = min(seq, LANE_GROUP)
    assert seq % tk == 0
    blk = pl.BlockSpec((seq, 2 * HEAD), lambda bi, pi: (bi, pi))
    blk_t = pl.BlockSpec((None, 2 * HEAD, seq), lambda bi, pi: (bi, pi, 0))
    return pl.pallas_call(
        functools.partial(_attn_prompt_kernel, tk),
        grid=(b, npair),
        in_specs=[blk_t, blk_t, blk_t, pl.BlockSpec((None, seq, AUG_LANES), lambda bi, pi: (bi, 0, 0))],
        out_specs=blk,
        out_shape=jax.ShapeDtypeStruct((b * seq, d), BF16),
        scratch_shapes=[pltpu.VMEM((seq, 2 * HEAD + AUG_LANES), BF16),
                        pltpu.VMEM((2, HEAD + ONES_ROWS, seq), BF16),
                        pltpu.VMEM((2, 2, seq, tk), F32),
                        pltpu.VMEM((2, 2, 8, tk), F32),
                        pltpu.VMEM((2, 2, seq, tk), BF16)],
        compiler_params=_cparams(("arbitrary", "arbitrary")),
        name="fox_attn_prompt",
    )(qt, kt, vt, c_aug)


def _attn_sample_kernel(q_ref, kp_ref, vp_ref, kn_ref, vn_ref, ckp_ref, ckn_ref, o_ref):
    pw = 2 * HEAD
    heads = [(p, h) for p in range(q_ref.shape[1] // pw) for h in range(2)]
    first = _head_pair_masks((q_ref.shape[0], pw))
    qpos = lax.broadcasted_iota(jnp.int32, (kn_ref.shape[1], kn_ref.shape[1]), 0)
    kpos = lax.broadcasted_iota(jnp.int32, (kn_ref.shape[1], kn_ref.shape[1]), 1)
    sp, sn = {}, {}
    for p, h in heads:
        q = q_ref[:, p * pw:(p + 1) * pw]
        qh = jnp.where(first, q, jnp.zeros_like(q)) if h == 0 else jnp.where(first, jnp.zeros_like(q), q)
        kp = kp_ref[p * pw:(p + 1) * pw, :].astype(BF16)
        kn = kn_ref[p * pw:(p + 1) * pw, :].astype(BF16)
        sp[p, h] = _dot(qh, kp) - ckp_ref[2 * p + h:2 * p + h + 1, :] * LOG2E
        sn[p, h] = jnp.where(kpos <= qpos, _dot(qh, kn) - ckn_ref[2 * p + h:2 * p + h + 1, :] * LOG2E, -jnp.inf)
    pp, pn, ls = {}, {}, {}
    for c in heads:
        m = jnp.maximum(jnp.max(sp[c], axis=-1, keepdims=True), jnp.max(sn[c], axis=-1, keepdims=True))
        ep = jnp.exp2(sp[c] - m)
        en = jnp.exp2(sn[c] - m)
        ls[c] = jnp.sum(ep, axis=-1, keepdims=True) + jnp.sum(en, axis=-1, keepdims=True)
        pp[c] = ep.astype(BF16)
        pn[c] = en.astype(BF16)
    for p in range(q_ref.shape[1] // pw):
        vp = vp_ref[p * pw:(p + 1) * pw, :].astype(BF16)
        vn = vn_ref[p * pw:(p + 1) * pw, :].astype(BF16)
        outs = [(_dot_nt(pp[p, h], vp) + _dot_nt(pn[p, h], vn)) / ls[p, h] for h in range(2)]
        o_ref[:, p * pw:(p + 1) * pw] = jnp.where(first, outs[0], outs[1]).astype(o_ref.dtype)


def _attn_sample(q, kpt, vpt, knt, vnt, ckp_row, ckn_row, seq, past):
    n, d = q.shape
    b = n // seq
    npair = d // (2 * HEAD)
    pps = 4 if npair % 4 == 0 else 1
    ng = npair // pps
    ckp = ckp_row.reshape(b, ng, 2 * pps, past)
    ckn = ckn_row.reshape(b, ng, 2 * pps, seq)
    rows = pl.BlockSpec((seq, pps * 2 * HEAD), lambda bi, pi: (bi, pi))
    new = pl.BlockSpec((None, pps * 2 * HEAD, seq), lambda bi, pi: (bi, pi, 0))
    old = pl.BlockSpec((None, pps * 2 * HEAD, past), lambda bi, pi: (bi, pi, 0))
    return pl.pallas_call(
        _attn_sample_kernel,
        grid=(b, ng),
        in_specs=[rows, old, old, new, new,
                  pl.BlockSpec((None, None, 2 * pps, past), lambda bi, pi: (bi, pi, 0, 0)),
                  pl.BlockSpec((None, None, 2 * pps, seq), lambda bi, pi: (bi, pi, 0, 0))],
        out_specs=rows,
        out_shape=jax.ShapeDtypeStruct((n, d), BF16),
        compiler_params=_cparams(("arbitrary", "arbitrary")),
        name="fox_attn_sample",
    )(q, kpt, vpt, knt, vnt, ckp, ckn)


def _pack_state(s):
    b, h, v, k = s.shape
    return s.reshape(b, h // HEADS_PER_GROUP, HEADS_PER_GROUP, v, k).transpose(0, 1, 3, 2, 4).reshape(
        b, h // HEADS_PER_GROUP, v, HEADS_PER_GROUP * k)


def _unpack_state(sc):
    b, g, v, gk = sc.shape
    k = gk // HEADS_PER_GROUP
    return sc.reshape(b, g, v, HEADS_PER_GROUP, k).transpose(0, 1, 3, 2, 4).reshape(b, g * HEADS_PER_GROUP, v, k)


def _trunk(x, shift0, wkv0, past, w):
    b, t, d = x.shape
    n = b * t
    h = x.reshape(n, d)
    n_a = w["w_r"].shape[0]
    new_shift, new_wkv = [], []
    for l in range(n_a):
        r, lw, k, v, av, bv, g, xl = _rwkv_pre(
            h, shift0[l].reshape(b, 1, d), t, w["ln1_g"][l], w["mu"][l], w["w_r"][l], w["w_k"][l], w["w_v"][l],
            w["w1"][l], w["w2"][l], w["a1"][l], w["a2"][l], w["g1"][l], w["g2"][l],
            w["w0"][l], w["a0"][l], w["k_k"][l], w["k_a"][l])
        tp = -(-t // CHUNK) * CHUNK
        scan_in = (r, lw, k, v, av, bv)
        if tp != t:
            scan_in = tuple(jnp.pad(z.reshape(b, t, d), ((0, 0), (0, tp - t), (0, 0))).reshape(b * tp, d)
                            for z in scan_in)
        y, s_out = _wkv_scan(*scan_in, _pack_state(wkv0[l]), tp)
        if tp != t:
            y = y.reshape(b, tp, d)[:, :t].reshape(n, d)
        h = _ffn(h, w["ln2_g"][l], w["w_gate"][l], w["w_up"][l], w["w_down"][l],
                 mix=(y, r, k, v, g, w["lnx_g"][l], w["lnx_b"][l], w["r_k"][l], w["w_o"][l]))
        new_shift.append(xl.reshape(b, d))
        new_wkv.append(_unpack_state(s_out))

    nh = d // HEAD
    n_b = w["w_qt"].shape[0]
    scale = HEAD ** -0.5 * LOG2E
    kt_new, vt_new, logf, qt = _kvq(h, t, w["kv_g"], w["ln1_g"][n_a], w["w_kvf_kt"], w["w_kvf_vt"], w["w_kvf_ft"],
                                    w["b_f"], w["w_qt"][0], scale)
    zero_carry = jnp.zeros((b, nh, 1), F32)
    if past is None:
        _, c_aug, _ = _cumsum(logf, zero_carry, True)
    else:
        pk, pv, plf = past
        p = pk.shape[1]
        pkt = jnp.transpose(pk, (0, 2, 3, 1)).reshape(b, d, p).astype(F32)
        pvt = jnp.transpose(pv, (0, 2, 3, 1)).reshape(b, d, p).astype(F32)
        cp_row, cp_last = _cumsum(jnp.transpose(plf, (0, 2, 1)).astype(F32), zero_carry, False)
        c_row, _ = _cumsum(logf, cp_last, False)
    for j in range(n_b):
        l = n_a + j
        if j > 0:
            qt = _q_only(h, t, w["ln1_g"][l], w["w_qt"][j], scale)
        if past is None:
            o = _attn_prompt(qt, kt_new, vt_new, c_aug)
        else:
            q_rows = jnp.transpose(qt, (0, 2, 1)).reshape(n, d)
            o = _attn_sample(q_rows, pkt, pvt, kt_new, vt_new, cp_row, c_row, t, p)
        last = j == n_b - 1
        h = _ffn(h, w["ln2_g"][l], w["w_gate"][l], w["w_up"][l], w["w_down"][l],
                 w["final_g"] if last else None, proj=(o, w["w_ob"][j]))
    y_out = h
    k_new = jnp.transpose(kt_new.reshape(b, nh, HEAD, t), (0, 3, 1, 2))
    v_new = jnp.transpose(vt_new.reshape(b, nh, HEAD, t), (0, 3, 1, 2))
    return (y_out.reshape(b, t, d), jnp.stack(new_shift), jnp.stack(new_wkv),
            k_new, v_new, jnp.transpose(logf, (0, 2, 1)))


def _q_only_kernel(scale, h_ref, ln_ref, wqt_ref, qt_out):
    qt_out[...] = (_dot_nt(wqt_ref[...], _rms(h_ref[...], ln_ref[...]).astype(BF16)) * scale).astype(BF16)


def _q_only(h, seq, ln, wqt, scale):
    n, d = h.shape
    tm = _row_tile(n, seq, ROWS_MIX)
    tps = seq // tm
    return pl.pallas_call(
        functools.partial(_q_only_kernel, scale),
        grid=(n // tm,),
        in_specs=[pl.BlockSpec((tm, d), lambda i: (i, 0)), _full((1, d)), _full(wqt.shape)],
        out_specs=pl.BlockSpec((None, d, tm), lambda i: (i // tps, 0, i % tps)),
        out_shape=jax.ShapeDtypeStruct((n // seq, d, seq), BF16),
        compiler_params=_cparams(("arbitrary",)),
        name="q_proj",
    )(h, ln, wqt)


def kernel(x_prompt, x_sample, state_shift, state_wkv, cache_k, cache_v, cache_logf, ln1_g, ln2_g, w_gate, w_up, w_down, mu, w_r, w_k, w_v, w_o, w0, w1, w2, a0, a1, a2, g1, g2, k_k, k_a, r_k, lnx_g, lnx_b, kv_g, w_kvf, b_f, w_q, w_ob, final_g):
    d = x_prompt.shape[-1]
    nh = d // HEAD
    n_a = w_r.shape[0]
    bf = lambda z: z.astype(BF16)
    vec = lambda z: z.reshape(z.shape[:-1] + (1, d)).astype(F32)
    w = dict(
        ln1_g=vec(ln1_g), ln2_g=vec(ln2_g), w_gate=bf(w_gate), w_up=bf(w_up), w_down=bf(w_down),
        mu=mu.astype(F32), w_r=bf(w_r), w_k=bf(w_k), w_v=bf(w_v), w_o=bf(w_o),
        w0=vec(w0), w1=bf(w1), w2=bf(w2), a0=vec(a0), a1=bf(a1), a2=bf(a2), g1=bf(g1), g2=bf(g2),
        k_k=vec(k_k), k_a=vec(k_a),
        r_k=r_k.reshape(n_a, 1, d).astype(F32),
        lnx_g=vec(lnx_g), lnx_b=vec(lnx_b), kv_g=vec(kv_g),
        w_kvf_kt=bf(w_kvf[:, :d].T), w_kvf_vt=bf(w_kvf[:, d:2 * d].T), w_kvf_ft=bf(w_kvf[:, 2 * d:].T),
        b_f=b_f.reshape(nh, 1).astype(F32), w_qt=bf(jnp.swapaxes(w_q, 1, 2)), w_ob=bf(w_ob), final_g=vec(final_g),
    )
    bp = x_prompt.shape[0]
    dt = x_prompt.dtype
    y_p, shift_p, wkv_p, k_p, v_p, lf_p = _trunk(
        x_prompt, jnp.zeros((n_a, bp, d), dt), jnp.zeros((n_a, bp, nh, HEAD, HEAD), dt), None, w)
    y_s, shift_s, wkv_s, k_s, v_s, lf_s = _trunk(
        x_sample, state_shift, state_wkv, (cache_k, cache_v, cache_logf), w)
    return (y_p, y_s, shift_p, wkv_p, k_p, v_p, lf_p, shift_s, wkv_s, k_s, v_s, lf_s)
```
